```python
import math
import jax
import jax.numpy as jnp
from jax import lax
import numpy as np

D_MODEL = 2048
BATCH = 16
SEQ = 2048
DEPTH = 1
DEC_BATCH = 8
DEC_SEQ = 16
PAST_LEN = 4096

CHUNK = 64
CONV_WIDTH = 4
RMS_EPS = 1e-6
SSM_INNER = D_MODEL
SSM_HEAD_DIM = 64
SSM_HEADS = SSM_INNER // SSM_HEAD_DIM
SSM_GROUPS = 4
SSM_HEADS_PER_GROUP = SSM_HEADS // SSM_GROUPS
SSM_STATE = 128
SSM_CONV_DIM = SSM_INNER + 2 * SSM_GROUPS * SSM_STATE
LRU_WIDTH = D_MODEL
LRU_BLOCKS = 16
LRU_BLOCK_WIDTH = LRU_WIDTH // LRU_BLOCKS
LRU_C = 8.0
N_EXPERT_GROUPS = 4
EXPERTS_PER_GROUP = 8
N_EXPERTS = N_EXPERT_GROUPS * EXPERTS_PER_GROUP
TOP_K = 2
EXPERT_FF = D_MODEL // 4
MOE_BLOCK = 256
PLE_DIM = 256
OFF_Z = SSM_INNER
OFF_XBC = OFF_Z + SSM_CONV_DIM
OFF_DT = OFF_XBC + SSM_HEADS
OFF_GY = OFF_DT + LRU_WIDTH
OFF_XL = OFF_GY + LRU_WIDTH
OFF_GM = OFF_XL + D_MODEL
IN_DIM = OFF_GM + D_MODEL

kernel_name = 'hybrid_ssd_rglru_hiermoe_stream_step'


def _rms_norm(x, g):
    x32 = x.astype(jnp.float32)
    y = x32 * lax.rsqrt(jnp.mean(x32 * x32, axis=-1, keepdims=True) + RMS_EPS)
    return (y * g.astype(jnp.float32)).astype(x.dtype)


def _causal_conv(x, buf, w, bias):
    xp = jnp.concatenate([buf.astype(x.dtype), x], axis=1)
    y = lax.conv_general_dilated(xp, w[:, None, :].astype(x.dtype), window_strides=(1,), padding='VALID',
                                 dimension_numbers=('NWC', 'WIO', 'NWC'), feature_group_count=x.shape[-1])
    return y + bias.astype(x.dtype), xp[:, -(CONV_WIDTH - 1):]


def _ssd(x, dt, a_neg, bm, cm, h0):
    f32 = jnp.float32
    b, L = x.shape[:2]
    G, HG, P, N = SSM_GROUPS, SSM_HEADS_PER_GROUP, SSM_HEAD_DIM, SSM_STATE
    pad = (-L) % CHUNK
    c = (L + pad) // CHUNK

    def chunked(t):
        t = jnp.pad(t.astype(f32), [(0, 0), (0, pad)] + [(0, 0)] * (t.ndim - 2))
        return t.reshape((b, c, CHUNK) + t.shape[2:])

    xc = chunked(x).reshape(b, c, CHUNK, G, HG, P)
    dtc = chunked(dt).reshape(b, c, CHUNK, G, HG)
    bc = chunked(bm)
    cc = chunked(cm)
    a_cum = jnp.cumsum(dtc * a_neg.astype(f32).reshape(G, HG), axis=2)
    causal = jnp.tril(jnp.ones((CHUNK, CHUNK), dtype=bool))[None, None, :, :, None, None]
    seg = a_cum[:, :, :, None] - a_cum[:, :, None, :]
    decay_in = jnp.exp(jnp.where(causal, seg, -jnp.inf))
    xdt = xc * dtc[..., None]
    cb = jnp.einsum('bclgn,bcsgn->bclsg', cc, bc)
    y_diag = jnp.einsum('bclsg,bclsgh,bcsghp->bclghp', cb, decay_in, xdt)
    decay_out = jnp.exp(a_cum[:, :, -1:] - a_cum)
    chunk_states = jnp.einsum('bcsgn,bcsgh,bcsghp->bcghpn', bc, decay_out, xdt)
    chunk_decay = jnp.exp(a_cum[:, :, -1])

    def step(h, inp):
        dec, st = inp
        return dec[..., None, None] * h + st, h

    h_last, h_prev = lax.scan(step, h0.astype(f32).reshape(b, G, HG, P, N),
                              (jnp.moveaxis(chunk_decay, 1, 0), jnp.moveaxis(chunk_states, 1, 0)))
    h_prev = jnp.moveaxis(h_prev, 0, 1)
    y_off = jnp.einsum('bclgn,bcghpn,bclgh->bclghp', cc, h_prev, jnp.exp(a_cum))
    y = (y_diag + y_off).reshape(b, c * CHUNK, SSM_HEADS, P)[:, :L]
    return y, h_last.reshape(b, SSM_HEADS, P, N)


def _rg_lru(x, h0, w_a, b_a, w_x, b_x, lam):
    f32 = jnp.float32
    b, L, _ = x.shape
    x32 = x.astype(f32)
    xb = x32.reshape(b, L, LRU_BLOCKS, LRU_BLOCK_WIDTH)
    r = jax.nn.sigmoid(jnp.einsum('blnk,nkj->blnj', xb, w_a.astype(f32)).reshape(b, L, LRU_WIDTH) + b_a.astype(f32))
    i = jax.nn.sigmoid(jnp.einsum('blnk,nkj->blnj', xb, w_x.astype(f32)).reshape(b, L, LRU_WIDTH) + b_x.astype(f32))
    log_a = -LRU_C * r * jax.nn.softplus(-lam.astype(f32))
    a = jnp.exp(log_a)
    u = jnp.sqrt(-jnp.expm1(2.0 * log_a)) * (i * x32)
    u = u.at[:, 0].add(a[:, 0] * h0.astype(f32))

    def combine(left, right):
        a_l, u_l = left
        a_r, u_r = right
        return a_l * a_r, a_r * u_l + u_r

    _, h = lax.associative_scan(combine, (a, u), axis=1)
    return h, h[:, -1]


def _hier_moe(x2, w_rg, b_rg, w_re, b_re, w_gate, w_up, w_down):
    f32 = jnp.float32
    T = x2.shape[0]
    x32 = x2.astype(f32)
    g_prob = jax.nn.softmax(x32 @ w_rg.astype(f32) + b_rg.astype(f32), axis=-1)
    g_w, g_idx = lax.top_k(g_prob, 1)
    e_logit = (x32 @ w_re.astype(f32) + b_re.astype(f32)).reshape(T, N_EXPERT_GROUPS, EXPERTS_PER_GROUP)
    e_logit = jnp.take_along_axis(e_logit, g_idx[:, :, None], axis=1)[:, 0]
    e_w, e_local = lax.top_k(jax.nn.softmax(e_logit, axis=-1), TOP_K)
    e_w = e_w / jnp.sum(e_w, axis=-1, keepdims=True)
    weight = (g_w * e_w).reshape(-1)
    expert = (g_idx * EXPERTS_PER_GROUP + e_local).reshape(-1).astype(jnp.int32)
    n_assign = T * TOP_K
    blk = max(8, min(MOE_BLOCK, n_assign // N_EXPERTS))
    n_blocks = -(-n_assign // blk) + N_EXPERTS
    order = jnp.argsort(expert)
    sorted_e = expert[order]
    token = (order // TOP_K).astype(jnp.int32)
    counts = jnp.bincount(expert, length=N_EXPERTS)
    padded = (counts + blk - 1) // blk * blk
    pad_end = jnp.cumsum(padded)
    pad_start = pad_end - padded
    seg_start = jnp.cumsum(counts) - counts
    dest = pad_start[sorted_e] + jnp.arange(n_assign) - seg_start[sorted_e]
    row_token = jnp.zeros((n_blocks * blk,), jnp.int32).at[dest].set(token)
    block_expert = jnp.minimum(jnp.searchsorted(pad_end, jnp.arange(n_blocks) * blk, side='right'), N_EXPERTS - 1)

    def expert_block(args):
        rows, e = args
        xb = x2[rows]
        hdn = jax.nn.silu(xb @ w_gate[e]) * (xb @ w_up[e])
        return hdn @ w_down[e]

    ys = lax.map(expert_block, (row_token.reshape(n_blocks, blk), block_expert)).reshape(n_blocks * blk, -1)
    contrib = ys[dest] * weight[order][:, None].astype(ys.dtype)
    return jnp.zeros_like(x2).at[token].add(contrib)


def _layer(x, p, ssm0, sconv0, lru0, lconv0, lp):
    f32 = jnp.float32
    b, L, _ = x.shape
    xn = _rms_norm(x, lp['g_mix'])
    proj = xn @ lp['w_in']
    z, xbc, dt_raw, gate_y, x_lru, gate_m, gate_r = jnp.split(proj, [OFF_Z, OFF_XBC, OFF_DT, OFF_GY, OFF_XL, OFF_GM], axis=-1)
    xbc, sconv_new = _causal_conv(xbc, sconv0, lp['conv_m_w'], lp['conv_m_b'])
    xbc = jax.nn.silu(xbc)
    xs, bm, cm = jnp.split(xbc, [SSM_INNER, SSM_INNER + SSM_GROUPS * SSM_STATE], axis=-1)
    xs = xs.reshape(b, L, SSM_HEADS, SSM_HEAD_DIM)
    dt = jax.nn.softplus(dt_raw.astype(f32) + lp['dt_bias'].astype(f32))
    y_ssm, ssm_new = _ssd(xs, dt, -jnp.exp(lp['a_log'].astype(f32)),
                          bm.reshape(b, L, SSM_GROUPS, SSM_STATE), cm.reshape(b, L, SSM_GROUPS, SSM_STATE), ssm0)
    y_ssm = y_ssm + lp['d_skip'].astype(f32)[:, None] * xs.astype(f32)
    y_ssm = _rms_norm(y_ssm.reshape(b, L, SSM_INNER) * jax.nn.silu(z.astype(f32)), lp['g_ssm']).astype(x.dtype)
    o_m = y_ssm @ lp['w_out_m']
    xl, lconv_new = _causal_conv(x_lru, lconv0, lp['conv_r_w'], lp['conv_r_b'])
    h, lru_new = _rg_lru(xl, lru0, lp['w_lru_a'], lp['b_lru_a'], lp['w_lru_x'], lp['b_lru_x'], lp['lru_lambda'])
    o_r = (h.astype(x.dtype) * jax.nn.gelu(gate_y)) @ lp['w_out_r']
    mixed = jax.nn.sigmoid(gate_m) * o_m + jax.nn.sigmoid(gate_r) * o_r
    x = x + mixed @ lp['w_o']
    xn2 = _rms_norm(x, lp['g_ffn'])
    x = x + _hier_moe(xn2.reshape(b * L, D_MODEL), lp['w_router_group'], lp['b_router_group'],
                      lp['w_router_expert'], lp['b_router_expert'], lp['w_exp_gate'], lp['w_exp_up'],
                      lp['w_exp_down']).reshape(b, L, D_MODEL)
    ple_gate = jax.nn.sigmoid(_rms_norm(x, lp['g_ple']) @ lp['w_ple_gate'])
    x = x + ple_gate * (p.astype(x.dtype) @ lp['w_ple'])
    return x, ssm_new, sconv_new, lru_new, lconv_new


def setup_inputs(seed: int = 0) -> dict:
    key = jax.random.key(seed)
    ks = iter(jax.random.split(key, 64))

    def nrm(shape, scale):
        return jax.random.normal(next(ks), shape, jnp.float32) * scale

    def gain(shape):
        return 1.0 + nrm(shape, 0.02)

    dt0 = jnp.exp(jax.random.uniform(next(ks), (DEPTH, SSM_HEADS), jnp.float32, math.log(1e-3), math.log(1e-1)))
    a0 = jax.random.uniform(next(ks), (DEPTH, LRU_WIDTH), jnp.float32, 0.9, 0.999)
    return {
        'x_prompt': nrm((BATCH, SEQ, D_MODEL), 1.0),
        'x_sample': nrm((DEC_BATCH, DEC_SEQ, D_MODEL), 1.0),
        'state_ssm': nrm((DEPTH, DEC_BATCH, SSM_HEADS, SSM_HEAD_DIM, SSM_STATE), 0.5),
        'state_ssm_conv': nrm((DEPTH, DEC_BATCH, CONV_WIDTH - 1, SSM_CONV_DIM), 1.0),
        'state_lru': nrm((DEPTH, DEC_BATCH, LRU_WIDTH), 0.5),
        'state_lru_conv': nrm((DEPTH, DEC_BATCH, CONV_WIDTH - 1, LRU_WIDTH), 1.0),
        'p_prompt': nrm((DEPTH, BATCH, SEQ, PLE_DIM), 1.0),
        'p_sample': nrm((DEPTH, DEC_BATCH, DEC_SEQ, PLE_DIM), 1.0),
        'g_mix': gain((DEPTH, D_MODEL)),
        'w_in': nrm((DEPTH, D_MODEL, IN_DIM), D_MODEL ** -0.5),
        'conv_m_w': nrm((DEPTH, CONV_WIDTH, SSM_CONV_DIM), 0.5),
        'conv_m_b': nrm((DEPTH, SSM_CONV_DIM), 0.01),
        'dt_bias': dt0 + jnp.log(-jnp.expm1(-dt0)),
        'a_log': jnp.log(jax.random.uniform(next(ks), (DEPTH, SSM_HEADS), jnp.float32, 1.0, 16.0)),
        'd_skip': gain((DEPTH, SSM_HEADS)),
        'g_ssm': gain((DEPTH, SSM_INNER)),
        'w_out_m': nrm((DEPTH, SSM_INNER, D_MODEL), SSM_INNER ** -0.5),
        'conv_r_w': nrm((DEPTH, CONV_WIDTH, LRU_WIDTH), 0.5),
        'conv_r_b': nrm((DEPTH, LRU_WIDTH), 0.01),
        'w_lru_a': nrm((DEPTH, LRU_BLOCKS, LRU_BLOCK_WIDTH, LRU_BLOCK_WIDTH), LRU_BLOCK_WIDTH ** -0.5),
        'b_lru_a': nrm((DEPTH, LRU_WIDTH), 0.01),
        'w_lru_x': nrm((DEPTH, LRU_BLOCKS, LRU_BLOCK_WIDTH, LRU_BLOCK_WIDTH), LRU_BLOCK_WIDTH ** -0.5),
        'b_lru_x': nrm((DEPTH, LRU_WIDTH), 0.01),
        'lru_lambda': jnp.log(a0) - jnp.log1p(-a0),
        'w_out_r': nrm((DEPTH, LRU_WIDTH, D_MODEL), LRU_WIDTH ** -0.5),
        'w_o': nrm((DEPTH, D_MODEL, D_MODEL), D_MODEL ** -0.5),
        'g_ffn': gain((DEPTH, D_MODEL)),
        'w_router_group': nrm((DEPTH, D_MODEL, N_EXPERT_GROUPS), D_MODEL ** -0.5),
        'b_router_group': nrm((DEPTH, N_EXPERT_GROUPS), 0.01),
        'w_router_expert': nrm((DEPTH, D_MODEL, N_EXPERTS), D_MODEL ** -0.5),
        'b_router_expert': nrm((DEPTH, N_EXPERTS), 0.01),
        'w_exp_gate': nrm((DEPTH, N_EXPERTS, D_MODEL, EXPERT_FF), D_MODEL ** -0.5),
        'w_exp_up': nrm((DEPTH, N_EXPERTS, D_MODEL, EXPERT_FF), D_MODEL ** -0.5),
        'w_exp_down': nrm((DEPTH, N_EXPERTS, EXPERT_FF, D_MODEL), EXPERT_FF ** -0.5),
        'g_ple': gain((DEPTH, D_MODEL)),
        'w_ple_gate': nrm((DEPTH, D_MODEL, D_MODEL), D_MODEL ** -0.5),
        'w_ple': nrm((DEPTH, PLE_DIM, D_MODEL), PLE_DIM ** -0.5),
        'g_final': gain((D_MODEL,)),
    }


def reference(x_prompt, x_sample, state_ssm, state_ssm_conv, state_lru, state_lru_conv, p_prompt, p_sample,
              g_mix, w_in, conv_m_w, conv_m_b, dt_bias, a_log, d_skip, g_ssm, w_out_m,
              conv_r_w, conv_r_b, w_lru_a, b_lru_a, w_lru_x, b_lru_x, lru_lambda, w_out_r, w_o,
              g_ffn, w_router_group, b_router_group, w_router_expert, b_router_expert,
              w_exp_gate, w_exp_up, w_exp_down, g_ple, w_ple_gate, w_ple, g_final):
    bp = x_prompt.shape[0]
    hp = x_prompt
    hs = x_sample
    ssm_p, sconv_p, lru_p, lconv_p = [], [], [], []
    ssm_s, sconv_s, lru_s, lconv_s = [], [], [], []
    for i in range(DEPTH):
        lp = {
            'g_mix': g_mix[i], 'w_in': w_in[i], 'conv_m_w': conv_m_w[i], 'conv_m_b': conv_m_b[i],
            'dt_bias': dt_bias[i], 'a_log': a_log[i], 'd_skip': d_skip[i], 'g_ssm': g_ssm[i],
            'w_out_m': w_out_m[i], 'conv_r_w': conv_r_w[i], 'conv_r_b': conv_r_b[i],
            'w_lru_a': w_lru_a[i], 'b_lru_a': b_lru_a[i], 'w_lru_x': w_lru_x[i], 'b_lru_x': b_lru_x[i],
            'lru_lambda': lru_lambda[i], 'w_out_r': w_out_r[i], 'w_o': w_o[i], 'g_ffn': g_ffn[i],
            'w_router_group': w_router_group[i], 'b_router_group': b_router_group[i],
            'w_router_expert': w_router_expert[i], 'b_router_expert': b_router_expert[i],
            'w_exp_gate': w_exp_gate[i], 'w_exp_up': w_exp_up[i], 'w_exp_down': w_exp_down[i],
            'g_ple': g_ple[i], 'w_ple_gate': w_ple_gate[i], 'w_ple': w_ple[i],
        }
        zero_ssm = jnp.zeros((bp, SSM_HEADS, SSM_HEAD_DIM, SSM_STATE), jnp.float32)
        zero_sconv = jnp.zeros((bp, CONV_WIDTH - 1, SSM_CONV_DIM), x_prompt.dtype)
        zero_lru = jnp.zeros((bp, LRU_WIDTH), jnp.float32)
        zero_lconv = jnp.zeros((bp, CONV_WIDTH - 1, LRU_WIDTH), x_prompt.dtype)
        hp, a1, a2, a3, a4 = _layer(hp, p_prompt[i], zero_ssm, zero_sconv, zero_lru, zero_lconv, lp)
        ssm_p.append(a1)
        sconv_p.append(a2)
        lru_p.append(a3)
        lconv_p.append(a4)
        hs, b1, b2, b3, b4 = _layer(hs, p_sample[i], state_ssm[i], state_ssm_conv[i], state_lru[i], state_lru_conv[i], lp)
        ssm_s.append(b1)
        sconv_s.append(b2)
        lru_s.append(b3)
        lconv_s.append(b4)
    y_prompt = _rms_norm(hp, g_final)
    y_sample = _rms_norm(hs, g_final)
    return (y_prompt, y_sample,
            jnp.stack(ssm_p), jnp.stack(sconv_p), jnp.stack(lru_p), jnp.stack(lconv_p),
            jnp.stack(ssm_s), jnp.stack(sconv_s), jnp.stack(lru_s), jnp.stack(lconv_s))
```

```python
import functools

import jax
import jax.numpy as jnp
from jax import lax
from jax.experimental import pallas as pl
from jax.experimental.pallas import tpu as pltpu

F32 = jnp.float32
BF16 = jnp.bfloat16

D_MODEL = 2048
CONV_WIDTH = 4
RMS_EPS = 1e-6
SSM_HEADS = 32
SSM_HEAD_DIM = 64
SSM_GROUPS = 4
SSM_HEADS_PER_GROUP = 8
SSM_STATE = 128
SSM_BC = 2 * SSM_GROUPS * SSM_STATE
SSM_CONV_DIM = D_MODEL + SSM_BC
LRU_BLOCKS = 16
LRU_BLOCK_WIDTH = 128
LRU_C = 8.0
N_EXPERT_GROUPS = 4
EXPERTS_PER_GROUP = 8
N_EXPERTS = 32
TOP_K = 2
EXPERT_FF = 512
PLE_DIM = 256
LANES = 128
SUBLANES = 8

SEC_XS, SEC_Z, SEC_GY, SEC_XL, SEC_GM, SEC_GR = range(6)
PROJ_BC_OFF = 6 * D_MODEL
PROJ_DT_OFF = PROJ_BC_OFF + SSM_BC
PROJ_W = PROJ_DT_OFF + LANES
PROJ_TN = 1920

SSD_CHUNK = 128
MOE_ROWS = 512
VMEM_LIMIT = 56 * 1024 * 1024


def _cparams(sem):
    return pltpu.CompilerParams(dimension_semantics=sem, vmem_limit_bytes=VMEM_LIMIT)


def _rms(x, g):
    return x * lax.rsqrt(jnp.mean(x * x, axis=-1, keepdims=True) + RMS_EPS) * g


def _softplus(x):
    return jnp.maximum(x, 0.0) + jnp.log1p(jnp.exp(-jnp.abs(x)))


def _silu(x):
    return x * jax.nn.sigmoid(x)


def _gelu_tanh(x):
    return 0.5 * x * (1.0 + jnp.tanh(0.7978845608028654 * (x + 0.044715 * (x * x * x))))


def _split3(v):
    hi = v.astype(BF16)
    r1 = v - hi.astype(F32)
    mid = r1.astype(BF16)
    lo = (r1 - mid.astype(F32)).astype(BF16)
    return hi, mid, lo


def _dot(a, b):
    return jnp.dot(a, b, preferred_element_type=F32)


def _dot_exact_rhs(v, m_bf16):
    hi, mid, lo = _split3(v)
    return _dot(hi, m_bf16) + _dot(mid, m_bf16) + _dot(lo, m_bf16)


def _dot_exact_lhs(m_bf16, v):
    hi, mid, lo = _split3(v)
    return _dot(m_bf16, hi) + _dot(m_bf16, mid) + _dot(m_bf16, lo)


def _proj_kernel(x_ref, g_ref, w_ref, o_ref):
    xn = _rms(x_ref[...], g_ref[...]).astype(BF16)
    o_ref[...] = _dot(xn, w_ref[...])


def _proj(x2d, g_mix, w_in_r, tm):
    t = x2d.shape[0]
    return pl.pallas_call(
        _proj_kernel,
        grid=(PROJ_W // PROJ_TN, t // tm),
        in_specs=[
            pl.BlockSpec((tm, D_MODEL), lambda j, i: (i, 0)),
            pl.BlockSpec((1, D_MODEL), lambda j, i: (0, 0)),
            pl.BlockSpec((D_MODEL, PROJ_TN), lambda j, i: (0, j)),
        ],
        out_specs=pl.BlockSpec((tm, PROJ_TN), lambda j, i: (i, j)),
        out_shape=jax.ShapeDtypeStruct((t, PROJ_W), F32),
        compiler_params=_cparams(("arbitrary", "arbitrary")),
        name="proj",
    )(x2d, g_mix, w_in_r)


def _conv_tile(cbuf, lt, c0, c1, cw_ref, cb_ref):
    acc = cb_ref[:, c0:c1]
    for k in range(CONV_WIDTH):
        acc = acc + cbuf[SUBLANES - (CONV_WIDTH - 1) + k:SUBLANES - (CONV_WIDTH - 1) + k + lt, c0:c1] * cw_ref[k:k + 1, c0:c1]
    return acc


def _ssd_kernel(xs_ref, bc_ref, dt_ref, z_ref, s0_ref, c0_ref, cw_ref, cb_ref, dtb_ref, alog_ref, dskip_ref,
                gssm_ref, rexp_ref, tril_ref,
                y_ref, snew_ref, cnew_ref,
                cbuf, xact, bcact, yscr, state, *, lt):
    q = SSD_CHUNK
    lp = max(lt, q)
    l = pl.program_id(1)

    @pl.when(l == 0)
    def _():
        cbuf[0:SUBLANES, :] = c0_ref[...]
        state[...] = s0_ref[...].T

    cbuf[SUBLANES:SUBLANES + lt, 0:D_MODEL] = xs_ref[...]
    cbuf[SUBLANES:SUBLANES + lt, D_MODEL:SSM_CONV_DIM] = bc_ref[...]
    cw = 512
    for c0 in range(0, D_MODEL, cw):
        xact[:, c0:c0 + cw] = _silu(_conv_tile(cbuf, lt, c0, c0 + cw, cw_ref, cb_ref))
    for c0 in range(0, SSM_BC, cw):
        bcact[:, c0:c0 + cw] = _silu(_conv_tile(cbuf, lt, D_MODEL + c0, D_MODEL + c0 + cw, cw_ref, cb_ref))
    tail = cbuf[lt:lt + SUBLANES, :]
    cbuf[0:SUBLANES, :] = tail
    cnew_ref[...] = tail

    dt = _softplus(dt_ref[...] + dtb_ref[...])
    a_neg = -jnp.exp(alog_ref[...])
    da = dt * a_neg
    rexp = rexp_ref[...]
    dt_rep = _dot_exact_rhs(dt, rexp)
    tril = tril_ref[...]
    row_i = lax.broadcasted_iota(jnp.int32, (q, q), 0)
    col_i = lax.broadcasted_iota(jnp.int32, (q, q), 1)
    causal = row_i >= col_i
    lane_i = lax.broadcasted_iota(jnp.int32, (2 * q, LANES), 1)
    row2_i = lax.broadcasted_iota(jnp.int32, (2 * q, LANES), 0)
    pair_mask = (lane_i < SSM_HEAD_DIM) == (row2_i < q)

    def pad_rows(v):
        if lt == lp:
            return v
        return jnp.concatenate([v, jnp.zeros((lp - lt, v.shape[1]), v.dtype)], axis=0)

    for c in range(lp // q):
        r0 = c * q
        r1 = min(r0 + q, lt)
        da_c = pad_rows(da[r0:r1])
        acum = _dot_exact_lhs(tril, da_c)
        acum_t = acum.T
        acum_rep = _dot_exact_rhs(acum, rexp)
        last = acum_rep[q - 1:q, :]
        exp_a = jnp.exp(acum_rep)
        dec_out = jnp.exp(last - acum_rep)
        chunk_dec = jnp.exp(last)
        xa = pad_rows(xact[r0:r1, :])
        xdt = xa * pad_rows(dt_rep[r0:r1])
        xw = (xdt * dec_out).astype(BF16)
        xdt_b = xdt.astype(BF16)
        bca = pad_rows(bcact[r0:r1, :])
        for g in range(SSM_GROUPS):
            gw = SSM_HEADS_PER_GROUP * SSM_HEAD_DIM
            bg = bca[:, g * SSM_STATE:(g + 1) * SSM_STATE]
            cg = bca[:, SSM_GROUPS * SSM_STATE + g * SSM_STATE:SSM_GROUPS * SSM_STATE + (g + 1) * SSM_STATE]
            bg_b = bg.astype(BF16)
            cg_b = cg.astype(BF16)
            cb = lax.dot_general(cg_b, bg_b, (((1,), (1,)), ((), ())), preferred_element_type=F32)
            sg = state[:, g * gw:(g + 1) * gw]
            y_off = _dot(cg_b, sg.astype(BF16)) * exp_a[:, g * gw:(g + 1) * gw]
            for j in range(SSM_HEADS_PER_GROUP // 2):
                h0 = g * SSM_HEADS_PER_GROUP + 2 * j
                ms = []
                for h in (h0, h0 + 1):
                    seg = acum[:, h:h + 1] - acum_t[h:h + 1, :]
                    ms.append(cb * jnp.exp(jnp.where(causal, seg, -jnp.inf)))
                m = jnp.concatenate(ms, axis=1).astype(BF16)
                xp = xdt_b[:, h0 * SSM_HEAD_DIM:h0 * SSM_HEAD_DIM + LANES]
                rhs = jnp.where(pair_mask, jnp.concatenate([xp, xp], axis=0), jnp.zeros_like(xp[:1, :1]))
                yd = _dot(m, rhs)
                co = h0 * SSM_HEAD_DIM
                yv = yd + y_off[:, j * LANES:(j + 1) * LANES] + dskip_ref[:, co:co + LANES] * xa[:, co:co + LANES]
                yscr[r0:r1, co:co + LANES] = yv[0:r1 - r0]
            state[:, g * gw:(g + 1) * gw] = sg * chunk_dec[:, g * gw:(g + 1) * gw] + _dot(
                bg.T.astype(BF16), xw[:, g * gw:(g + 1) * gw])

    yz = yscr[...] * _silu(z_ref[...])
    y_ref[...] = _rms(yz, gssm_ref[...]).astype(BF16)

    @pl.when(l == pl.num_programs(1) - 1)
    def _():
        snew_ref[...] = state[...].T


def _ssd(proj3, s0, c0p, cw8, cb, dtb, alog, dskip_rep, gssm, rexp, tril, lt):
    b, L, _ = proj3.shape
    nl = L // lt
    kern = functools.partial(_ssd_kernel, lt=lt)
    const = lambda bi, li: (0, 0)
    return pl.pallas_call(
        kern,
        grid=(b, nl),
        in_specs=[
            pl.BlockSpec((None, lt, D_MODEL), lambda bi, li: (bi, li, SEC_XS)),
            pl.BlockSpec((None, lt, SSM_BC), lambda bi, li: (bi, li, PROJ_BC_OFF // SSM_BC)),
            pl.BlockSpec((None, lt, LANES), lambda bi, li: (bi, li, PROJ_DT_OFF // LANES)),
            pl.BlockSpec((None, lt, D_MODEL), lambda bi, li: (bi, li, SEC_Z)),
            pl.BlockSpec((None, D_MODEL, SSM_STATE), lambda bi, li: (bi, 0, 0)),
            pl.BlockSpec((None, SUBLANES, SSM_CONV_DIM), lambda bi, li: (bi, 0, 0)),
            pl.BlockSpec((SUBLANES, SSM_CONV_DIM), const),
            pl.BlockSpec((1, SSM_CONV_DIM), const),
            pl.BlockSpec((1, LANES), const),
            pl.BlockSpec((1, LANES), const),
            pl.BlockSpec((1, D_MODEL), const),
            pl.BlockSpec((1, D_MODEL), const),
            pl.BlockSpec((LANES, D_MODEL), const),
            pl.BlockSpec((SSD_CHUNK, SSD_CHUNK), const),
        ],
        out_specs=[
            pl.BlockSpec((None, lt, D_MODEL), lambda bi, li: (bi, li, 0)),
            pl.BlockSpec((None, D_MODEL, SSM_STATE), lambda bi, li: (bi, 0, 0)),
            pl.BlockSpec((None, SUBLANES, SSM_CONV_DIM), lambda bi, li: (bi, 0, 0)),
        ],
        out_shape=[
            jax.ShapeDtypeStruct((b, L, D_MODEL), BF16),
            jax.ShapeDtypeStruct((b, D_MODEL, SSM_STATE), F32),
            jax.ShapeDtypeStruct((b, SUBLANES, SSM_CONV_DIM), F32),
        ],
        scratch_shapes=[
            pltpu.VMEM((lt + SUBLANES, SSM_CONV_DIM), F32),
            pltpu.VMEM((lt, D_MODEL), F32),
            pltpu.VMEM((lt, SSM_BC), F32),
            pltpu.VMEM((lt, D_MODEL), F32),
            pltpu.VMEM((SSM_STATE, D_MODEL), F32),
        ],
        compiler_params=_cparams(("arbitrary", "arbitrary")),
        name="ssd",
    )(proj3, proj3, proj3, proj3, s0, c0p, cw8, cb, dtb, alog, dskip_rep, gssm, rexp, tril)


def _lru_kernel(xl_ref, gy_ref, h0_ref, c0_ref, cw_ref, cb_ref, wax_ref, ba_ref, bx_ref, lam_ref,
                hg_ref, hnew_ref, cnew_ref,
                cbuf, abuf, ubuf, hcar, *, lt, nseg):
    l = pl.program_id(1)
    seg = lt // nseg

    @pl.when(l == 0)
    def _():
        cbuf[0:SUBLANES, :] = c0_ref[...]
        hcar[...] = h0_ref[...]

    cbuf[SUBLANES:SUBLANES + lt, :] = xl_ref[...]
    for n in range(LRU_BLOCKS):
        c0 = n * LRU_BLOCK_WIDTH
        c1 = c0 + LRU_BLOCK_WIDTH
        xc = _conv_tile(cbuf, lt, c0, c1, cw_ref, cb_ref)
        pre = _dot(xc.astype(BF16), wax_ref[n])
        r = jax.nn.sigmoid(pre[:, 0:LRU_BLOCK_WIDTH] + ba_ref[:, c0:c1])
        i = jax.nn.sigmoid(pre[:, LRU_BLOCK_WIDTH:2 * LRU_BLOCK_WIDTH] + bx_ref[:, c0:c1])
        log_a = (-LRU_C) * r * _softplus(-lam_ref[:, c0:c1])
        a = jnp.exp(log_a)
        abuf[n] = a
        ubuf[n] = jnp.sqrt(jnp.tanh(-log_a) * (a * a + 1.0)) * (i * xc)
    tail = cbuf[lt:lt + SUBLANES, :]
    cbuf[0:SUBLANES, :] = tail
    cnew_ref[...] = tail

    def body(j, carry):
        idx = pl.ds(j, nseg, stride=seg) if nseg > 1 else pl.ds(j, 1)
        out = []
        for n in range(LRU_BLOCKS):
            hl, ap = carry[n]
            a_j = abuf[n, idx, :]
            hl = a_j * hl + ubuf[n, idx, :]
            ap = a_j * ap
            ubuf[n, idx, :] = hl
            abuf[n, idx, :] = ap
            out.append((hl, ap))
        return tuple(out)

    init = tuple((jnp.zeros((nseg, LRU_BLOCK_WIDTH), F32), jnp.ones((nseg, LRU_BLOCK_WIDTH), F32))
                 for _ in range(LRU_BLOCKS))
    ends = lax.fori_loop(0, seg, body, init)
    for n in range(LRU_BLOCKS):
        c0 = n * LRU_BLOCK_WIDTH
        c1 = c0 + LRU_BLOCK_WIDTH
        hl, ap = ends[n]
        carry = hcar[:, c0:c1]
        for s in range(nseg):
            rows = slice(s * seg, (s + 1) * seg)
            h = ubuf[n, rows, :] + abuf[n, rows, :] * carry
            hg_ref[rows, c0:c1] = (h * _gelu_tanh(gy_ref[rows, c0:c1])).astype(BF16)
            carry = hl[s:s + 1, :] + ap[s:s + 1, :] * carry
        hcar[:, c0:c1] = carry
        hnew_ref[:, c0:c1] = carry


def _lru(proj3, h0, c0p, cw8, cb, wax, ba, bx, lam, lt):
    b, L, _ = proj3.shape
    nl = L // lt
    nseg = SUBLANES if lt % (SUBLANES * SUBLANES) == 0 else 1
    kern = functools.partial(_lru_kernel, lt=lt, nseg=nseg)
    const = lambda bi, li: (0, 0)
    return pl.pallas_call(
        kern,
        grid=(b, nl),
        in_specs=[
            pl.BlockSpec((None, lt, D_MODEL), lambda bi, li: (bi, li, SEC_XL)),
            pl.BlockSpec((None, lt, D_MODEL), lambda bi, li: (bi, li, SEC_GY)),
            pl.BlockSpec((None, 1, D_MODEL), lambda bi, li: (bi, 0, 0)),
            pl.BlockSpec((None, SUBLANES, D_MODEL), lambda bi, li: (bi, 0, 0)),
            pl.BlockSpec((SUBLANES, D_MODEL), const),
            pl.BlockSpec((1, D_MODEL), const),
            pl.BlockSpec((LRU_BLOCKS, LRU_BLOCK_WIDTH, 2 * LRU_BLOCK_WIDTH), lambda bi, li: (0, 0, 0)),
            pl.BlockSpec((1, D_MODEL), const),
            pl.BlockSpec((1, D_MODEL), const),
            pl.BlockSpec((1, D_MODEL), const),
        ],
        out_specs=[
            pl.BlockSpec((None, lt, D_MODEL), lambda bi, li: (bi, li, 0)),
            pl.BlockSpec((None, 1, D_MODEL), lambda bi, li: (bi, 0, 0)),
            pl.BlockSpec((None, SUBLANES, D_MODEL), lambda bi, li: (bi, 0, 0)),
        ],
        out_shape=[
            jax.ShapeDtypeStruct((b, L, D_MODEL), BF16),
            jax.ShapeDtypeStruct((b, 1, D_MODEL), F32),
            jax.ShapeDtypeStruct((b, SUBLANES, D_MODEL), F32),
        ],
        scratch_shapes=[
            pltpu.VMEM((lt + SUBLANES, D_MODEL), F32),
            pltpu.VMEM((LRU_BLOCKS, lt, LRU_BLOCK_WIDTH), F32),
            pltpu.VMEM((LRU_BLOCKS, lt, LRU_BLOCK_WIDTH), F32),
            pltpu.VMEM((1, D_MODEL), F32),
        ],
        compiler_params=_cparams(("arbitrary", "arbitrary")),
        name="lru",
    )(proj3, proj3, h0, c0p, cw8, cb, wax, ba, bx, lam)


def _route(logits):
    neg = -1e30
    lane = lax.broadcasted_iota(jnp.int32, logits.shape, 1).astype(F32)
    is_g = lane < N_EXPERT_GROUPS
    gl = jnp.where(is_g, logits, neg)
    gmax = jnp.max(gl, axis=1, keepdims=True)
    gsum = jnp.sum(jnp.where(is_g, jnp.exp(gl - gmax), 0.0), axis=1, keepdims=True)
    g_w = 1.0 / gsum
    gidx = jnp.min(jnp.where(gl == gmax, lane, float(LANES)), axis=1, keepdims=True)
    lo = N_EXPERT_GROUPS + EXPERTS_PER_GROUP * gidx
    el = jnp.where((lane >= lo) & (lane < lo + EXPERTS_PER_GROUP), logits, neg)
    m1 = jnp.max(el, axis=1, keepdims=True)
    i1 = jnp.min(jnp.where(el == m1, lane, float(LANES)), axis=1, keepdims=True)
    el2 = jnp.where(lane == i1, neg, el)
    m2 = jnp.max(el2, axis=1, keepdims=True)
    i2 = jnp.min(jnp.where(el2 == m2, lane, float(LANES)), axis=1, keepdims=True)
    e2 = jnp.exp(m2 - m1)
    w1 = g_w / (1.0 + e2)
    w2 = g_w * e2 / (1.0 + e2)
    out = jnp.where(lane == 0, i1 - N_EXPERT_GROUPS, 0.0)
    out = jnp.where(lane == 1, i2 - N_EXPERT_GROUPS, out)
    out = jnp.where(lane == 2, w1, out)
    out = jnp.where(lane == 3, w2, out)
    return out


def _merge_kernel(x_ref, y_ref, hg_ref, gm_ref, gr_ref, wom_ref, wor_ref, wo_ref, gffn_ref, wr_ref, br_ref,
                  *rest, n_real):
    x1_ref, xn2_ref, route_ref = rest[-3:]
    i = pl.program_id(0)

    @pl.when(i < n_real)
    def _():
        o_m = _dot(y_ref[...], wom_ref[...])
        o_r = _dot(hg_ref[...], wor_ref[...])
        mixed = jax.nn.sigmoid(gm_ref[...]) * o_m + jax.nn.sigmoid(gr_ref[...]) * o_r
        x1 = x_ref[...] + _dot(mixed.astype(BF16), wo_ref[...])
        x1_ref[...] = x1
        xn2 = _rms(x1, gffn_ref[...])
        xn2_ref[...] = xn2
        logits = jnp.dot(xn2, wr_ref[...], preferred_element_type=F32, precision=lax.Precision.HIGHEST) + br_ref[...]
        route_ref[...] = _route(logits)

    @pl.when(i >= n_real)
    def _():
        xn2_ref[...] = jnp.zeros_like(xn2_ref)
        route_ref[...] = jnp.zeros_like(route_ref)


def _merge(x2d, y2d, hg2d, proj2d, wom, wor, wo, gffn, wr, br, tm, t_all, row_off, shared):
    t = x2d.shape[0]
    n_real = t // tm
    off = row_off // tm
    n_extra = 0
    if shared is None and t_all > t:
        assert row_off == 0 and t_all - t <= tm
        n_extra = 1
    const = lambda i: (0, 0)
    cl = lambda i: jnp.minimum(i, n_real - 1)
    wspec = pl.BlockSpec((D_MODEL, D_MODEL), const, pipeline_mode=pl.Buffered(1))
    in_specs = [
        pl.BlockSpec((tm, D_MODEL), lambda i: (cl(i), 0)),
        pl.BlockSpec((tm, D_MODEL), lambda i: (cl(i), 0)),
        pl.BlockSpec((tm, D_MODEL), lambda i: (cl(i), 0)),
        pl.BlockSpec((tm, D_MODEL), lambda i: (cl(i), SEC_GM)),
        pl.BlockSpec((tm, D_MODEL), lambda i: (cl(i), SEC_GR)),
        wspec, wspec, wspec,
        pl.BlockSpec((1, D_MODEL), const),
        pl.BlockSpec((D_MODEL, LANES), const),
        pl.BlockSpec((1, LANES), const),
    ]
    args = [x2d, y2d, hg2d, proj2d, proj2d, wom, wor, wo, gffn, wr, br]
    aliases = {}
    if shared is not None:
        in_specs += [pl.BlockSpec(memory_space=pl.ANY), pl.BlockSpec(memory_space=pl.ANY)]
        aliases = {len(args): 1, len(args) + 1: 2}
        args += list(shared)
    return pl.pallas_call(
        functools.partial(_merge_kernel, n_real=n_real),
        grid=(n_real + n_extra,),
        in_specs=in_specs,
        out_specs=[
            pl.BlockSpec((tm, D_MODEL), lambda i: (cl(i), 0)),
            pl.BlockSpec((tm, D_MODEL), lambda i: (i + off, 0)),
            pl.BlockSpec((tm, LANES), lambda i: (i + off, 0)),
        ],
        out_shape=[
            jax.ShapeDtypeStruct((t, D_MODEL), F32),
            jax.ShapeDtypeStruct((t_all, D_MODEL), F32),
            jax.ShapeDtypeStruct((t_all, LANES), F32),
        ],
        input_output_aliases=aliases,
        compiler_params=_cparams(("arbitrary",)),
        name="merge",
    )(*args)


def _moe_kernel(be_ref, nused_ref, tok_cur_ref, tok_nxt_ref, xn2_hbm, wg_ref, wu_ref, wd_ref, ys_ref, xbuf, sem):
    i = pl.program_id(0)
    n_used = nused_ref[0]

    def gather(tok_ref, slot):
        def issue(r, c):
            pltpu.make_async_copy(xn2_hbm.at[pl.ds(tok_ref[0, r], 1), :], xbuf.at[slot, pl.ds(r, 1), :],
                                  sem.at[slot]).start()
            return c
        lax.fori_loop(0, MOE_ROWS, issue, 0)

    @pl.when(jnp.logical_and(i == 0, n_used > 0))
    def _():
        gather(tok_cur_ref, 0)

    @pl.when(i + 1 < n_used)
    def _():
        gather(tok_nxt_ref, (i + 1) % 2)

    @pl.when(i < n_used)
    def _():
        slot = i % 2
        pltpu.make_async_copy(xn2_hbm.at[pl.ds(0, MOE_ROWS), :], xbuf.at[slot], sem.at[slot]).wait()
        xb = xbuf[slot].astype(BF16)
        hdn = _silu(_dot(xb, wg_ref[...])) * _dot(xb, wu_ref[...])
        ys_ref[...] = _dot(hdn.astype(BF16), wd_ref[...])

    @pl.when(i >= n_used)
    def _():
        ys_ref[...] = jnp.zeros_like(ys_ref)


def _moe(block_expert, n_used, row_token3, xn2_all, wg, wu, wd):
    n_blocks = row_token3.shape[0]
    grid_spec = pltpu.PrefetchScalarGridSpec(
        num_scalar_prefetch=2,
        grid=(n_blocks,),
        in_specs=[
            pl.BlockSpec((None, 1, MOE_ROWS), lambda i, be, nu: (i, 0, 0), memory_space=pltpu.SMEM),
            pl.BlockSpec((None, 1, MOE_ROWS), lambda i, be, nu: (jnp.minimum(i + 1, n_blocks - 1), 0, 0),
                         memory_space=pltpu.SMEM),
            pl.BlockSpec(memory_space=pl.ANY),
            pl.BlockSpec((None, D_MODEL, EXPERT_FF), lambda i, be, nu: (be[i], 0, 0)),
            pl.BlockSpec((None, D_MODEL, EXPERT_FF), lambda i, be, nu: (be[i], 0, 0)),
            pl.BlockSpec((None, EXPERT_FF, D_MODEL), lambda i, be, nu: (be[i], 0, 0)),
        ],
        out_specs=pl.BlockSpec((MOE_ROWS, D_MODEL), lambda i, be, nu: (i, 0)),
        scratch_shapes=[
            pltpu.VMEM((2, MOE_ROWS, D_MODEL), F32),
            pltpu.SemaphoreType.DMA((2,)),
        ],
    )
    return pl.pallas_call(
        _moe_kernel,
        grid_spec=grid_spec,
        out_shape=jax.ShapeDtypeStruct((n_blocks * MOE_ROWS, D_MODEL), F32),
        compiler_params=_cparams(("arbitrary",)),
        name="moe",
    )(block_expert, n_used, row_token3, row_token3, xn2_all, wg, wu, wd)


def _final_kernel(d_cur_ref, d_nxt_ref, ys_hbm, x1_ref, route_ref, p_ref, gple_ref, wpg_ref, wple_ref, gfin_ref,
                  o_ref, gbuf, sem, *, tm):
    i = pl.program_id(0)
    n = pl.num_programs(0)

    def gather(d_ref, slot):
        def issue(r, c):
            pltpu.make_async_copy(ys_hbm.at[pl.ds(d_ref[0, r], 1), :], gbuf.at[slot, pl.ds(r, 1), :],
                                  sem.at[slot]).start()
            return c
        lax.fori_loop(0, TOP_K * tm, issue, 0)

    @pl.when(i == 0)
    def _():
        gather(d_cur_ref, 0)

    @pl.when(i + 1 < n)
    def _():
        gather(d_nxt_ref, (i + 1) % 2)

    slot = i % 2
    pltpu.make_async_copy(ys_hbm.at[pl.ds(0, TOP_K * tm), :], gbuf.at[slot], sem.at[slot]).wait()
    route = route_ref[...]
    moe = route[:, 2:3] * gbuf[slot, 0:tm, :] + route[:, 3:4] * gbuf[slot, tm:2 * tm, :]
    x2 = x1_ref[...] + moe
    gate = jax.nn.sigmoid(_dot(_rms(x2, gple_ref[...]).astype(BF16), wpg_ref[...]))
    x3 = x2 + gate * _dot(p_ref[...].astype(BF16), wple_ref[...])
    o_ref[...] = _rms(x3, gfin_ref[...])


def _final(dest3, ys, x1, route_all, p2d, gple, wpg, wple, gfin, tm, row_off):
    t = x1.shape[0]
    n = t // tm
    off = row_off // tm
    const = lambda i: (0, 0)
    kern = functools.partial(_final_kernel, tm=tm)
    return pl.pallas_call(
        kern,
        grid=(n,),
        in_specs=[
            pl.BlockSpec((None, 1, TOP_K * tm), lambda i: (i, 0, 0), memory_space=pltpu.SMEM),
            pl.BlockSpec((None, 1, TOP_K * tm), lambda i: (jnp.minimum(i + 1, n - 1), 0, 0), memory_space=pltpu.SMEM),
            pl.BlockSpec(memory_space=pl.ANY),
            pl.BlockSpec((tm, D_MODEL), lambda i: (i, 0)),
            pl.BlockSpec((tm, LANES), lambda i: (i + off, 0)),
            pl.BlockSpec((tm, PLE_DIM), lambda i: (i, 0)),
            pl.BlockSpec((1, D_MODEL), const),
            pl.BlockSpec((D_MODEL, D_MODEL), const, pipeline_mode=pl.Buffered(1)),
            pl.BlockSpec((PLE_DIM, D_MODEL), const),
            pl.BlockSpec((1, D_MODEL), const),
        ],
        out_specs=pl.BlockSpec((tm, D_MODEL), lambda i: (i, 0)),
        out_shape=jax.ShapeDtypeStruct((t, D_MODEL), F32),
        scratch_shapes=[
            pltpu.VMEM((2, TOP_K * tm, D_MODEL), F32),
            pltpu.SemaphoreType.DMA((2,)),
        ],
        compiler_params=_cparams(("arbitrary",)),
        name="final",
    )(dest3, dest3, ys, x1, route_all, p2d, gple, wpg, wple, gfin)


def _pad_rows8(a, at_end=False):
    r = a.shape[-2]
    pad = [(0, 0)] * (a.ndim - 2) + ([(0, SUBLANES - r)] if at_end else [(SUBLANES - r, 0)]) + [(0, 0)]
    return jnp.pad(a, pad)


def _dispatch(route_all):
    t_all = route_all.shape[0]
    n_assign = t_all * TOP_K
    n_blocks = -(-n_assign // MOE_ROWS) + N_EXPERTS
    expert = route_all[:, 0:TOP_K].astype(jnp.int32).reshape(-1)
    onehot = (expert[:, None] == jnp.arange(N_EXPERTS, dtype=jnp.int32)[None, :]).astype(jnp.int32)
    csum = jnp.cumsum(onehot, axis=0)
    rank = jnp.sum(csum * onehot, axis=1) - 1
    counts = csum[-1]
    nblk = (counts + MOE_ROWS - 1) // MOE_ROWS
    blk_end = jnp.cumsum(nblk)
    blk_start = blk_end - nblk
    dest = blk_start[expert] * MOE_ROWS + rank
    token = jnp.arange(n_assign, dtype=jnp.int32) // TOP_K
    row_token = jnp.zeros((n_blocks * MOE_ROWS,), jnp.int32).at[dest].set(token)
    block_expert = jnp.minimum(jnp.searchsorted(blk_end, jnp.arange(n_blocks, dtype=jnp.int32), side='right'),
                               N_EXPERTS - 1).astype(jnp.int32)
    n_used = blk_end[-1:].astype(jnp.int32)
    return dest.reshape(t_all, TOP_K).astype(jnp.int32), row_token.reshape(n_blocks, 1, MOE_ROWS), block_expert, n_used


def _tile_dest(dest, tm):
    t = dest.shape[0]
    return dest.reshape(t // tm, tm, TOP_K).transpose(0, 2, 1).reshape(t // tm, 1, TOP_K * tm)


def kernel(x_prompt, x_sample, state_ssm, state_ssm_conv, state_lru, state_lru_conv, p_prompt, p_sample, g_mix, w_in, conv_m_w, conv_m_b, dt_bias, a_log, d_skip, g_ssm, w_out_m, conv_r_w, conv_r_b, w_lru_a, b_lru_a, w_lru_x, b_lru_x, lru_lambda, w_out_r, w_o, g_ffn, w_router_group, b_router_group, w_router_expert, b_router_expert, w_exp_gate, w_exp_up, w_exp_down, g_ple, w_ple_gate, w_ple, g_final):
    depth = w_in.shape[0]
    assert depth == 1, "one layer per step"
    bp, lp_, _ = x_prompt.shape
    bs, ls, _ = x_sample.shape
    tp, ts = bp * lp_, bs * ls
    t_all = tp + ts

    w = w_in[0]
    z_end, xbc_end, dt_end = D_MODEL, D_MODEL + SSM_CONV_DIM, D_MODEL + SSM_CONV_DIM + SSM_HEADS
    off_xl = dt_end + D_MODEL
    off_gm = off_xl + D_MODEL
    off_gr = off_gm + D_MODEL
    w_in_r = jnp.concatenate([
        w[:, z_end:z_end + D_MODEL],
        w[:, 0:z_end],
        w[:, dt_end:off_xl],
        w[:, off_xl:off_gm],
        w[:, off_gm:off_gr],
        w[:, off_gr:off_gr + D_MODEL],
        w[:, z_end + D_MODEL:xbc_end],
        w[:, xbc_end:dt_end],
        jnp.zeros((D_MODEL, LANES - SSM_HEADS), w.dtype),
    ], axis=1).astype(BF16)
    row = lambda v: v.reshape(1, -1).astype(F32)
    pad_lanes = lambda v: jnp.pad(v.reshape(1, -1).astype(F32), ((0, 0), (0, LANES - v.shape[-1])))
    cw_m = _pad_rows8(conv_m_w[0].astype(F32), at_end=True)
    cw_r = _pad_rows8(conv_r_w[0].astype(F32), at_end=True)
    dskip_rep = jnp.repeat(d_skip[0].astype(F32), SSM_HEAD_DIM).reshape(1, D_MODEL)
    rexp = (jnp.arange(D_MODEL)[None, :] // SSM_HEAD_DIM == jnp.arange(LANES)[:, None]).astype(BF16)
    tril = jnp.tril(jnp.ones((SSD_CHUNK, SSD_CHUNK), F32)).astype(BF16)
    wax = jnp.concatenate([w_lru_a[0], w_lru_x[0]], axis=-1).astype(BF16)
    wom, wor, wo = w_out_m[0].astype(BF16), w_out_r[0].astype(BF16), w_o[0].astype(BF16)
    wr = jnp.pad(jnp.concatenate([w_router_group[0], w_router_expert[0]], axis=1).astype(F32),
                 ((0, 0), (0, LANES - N_EXPERT_GROUPS - N_EXPERTS)))
    br = pad_lanes(jnp.concatenate([b_router_group[0], b_router_expert[0]]))
    wg, wu, wd = w_exp_gate[0].astype(BF16), w_exp_up[0].astype(BF16), w_exp_down[0].astype(BF16)
    wpg, wple = w_ple_gate[0].astype(BF16), w_ple[0].astype(BF16)

    def branches(x, ssm0, sconv0, lru0, lconv0, tm_proj, lt):
        b, L, _ = x.shape
        x2d = x.reshape(b * L, D_MODEL)
        proj = _proj(x2d, row(g_mix[0]), w_in_r, tm_proj)
        proj3 = proj.reshape(b, L, PROJ_W)
        y, ssm_new, sconv8 = _ssd(proj3, ssm0.reshape(b, D_MODEL, SSM_STATE).astype(F32), _pad_rows8(sconv0.astype(F32)),
                                  cw_m, row(conv_m_b[0]), pad_lanes(dt_bias[0]), pad_lanes(a_log[0]), dskip_rep,
                                  row(g_ssm[0]), rexp, tril, lt)
        hg, lru_new, lconv8 = _lru(proj3, lru0.reshape(b, 1, D_MODEL).astype(F32), _pad_rows8(lconv0.astype(F32)),
                                   cw_r, row(conv_r_b[0]), wax, row(b_lru_a[0]), row(b_lru_x[0]), row(lru_lambda[0]), lt)
        states = (ssm_new.reshape(1, b, SSM_HEADS, SSM_HEAD_DIM, SSM_STATE),
                  sconv8[None, :, SUBLANES - (CONV_WIDTH - 1):, :],
                  lru_new.reshape(1, b, D_MODEL),
                  lconv8[None, :, SUBLANES - (CONV_WIDTH - 1):, :])
        return x2d, proj, y.reshape(b * L, D_MODEL), hg.reshape(b * L, D_MODEL), states

    zeros = lambda *s: jnp.zeros(s, F32)
    lt_p = min(256, lp_)
    tm_p = min(256, tp)
    tm_s = min(128, ts)
    xp2, proj_p, y_p, hg_p, st_p = branches(
        x_prompt, zeros(bp, SSM_HEADS, SSM_HEAD_DIM, SSM_STATE), zeros(bp, CONV_WIDTH - 1, SSM_CONV_DIM),
        zeros(bp, D_MODEL), zeros(bp, CONV_WIDTH - 1, D_MODEL), min(512, tp), lt_p)
    xs2, proj_s, y_s, hg_s, st_s = branches(
        x_sample, state_ssm[0], state_ssm_conv[0], state_lru[0], state_lru_conv[0], tm_s, ls)

    x1_p, xn2_all, route_all = _merge(xp2, y_p, hg_p, proj_p, wom, wor, wo, row(g_ffn[0]), wr, br, tm_p, t_all, 0, None)
    x1_s, xn2_all, route_all = _merge(xs2, y_s, hg_s, proj_s, wom, wor, wo, row(g_ffn[0]), wr, br, tm_s, t_all, tp,
                                      (xn2_all, route_all))

    dest, row_token3, block_expert, n_used = _dispatch(route_all)
    ys = _moe(block_expert, n_used, row_token3, xn2_all, wg, wu, wd)

    fin = lambda x1, d, p, tm, off: _final(_tile_dest(d, tm), ys, x1, route_all, p.reshape(-1, PLE_DIM).astype(F32),
                                           row(g_ple[0]), wpg, wple, row(g_final), tm, off)
    y_prompt = fin(x1_p, dest[:tp], p_prompt[0], tm_p, 0).reshape(x_prompt.shape)
    y_sample = fin(x1_s, dest[tp:], p_sample[0], tm_s, tp).reshape(x_sample.shape)
    return (y_prompt, y_sample) + st_p + st_s
```

```python
import functools

import jax
import jax.numpy as jnp
from jax import lax
from jax.experimental import pallas as pl
from jax.experimental.pallas import tpu as pltpu

F32 = jnp.float32
BF16 = jnp.bfloat16

D_MODEL = 2048
CONV_WIDTH = 4
RMS_EPS = 1e-6
SSM_HEADS = 32
SSM_HEAD_DIM = 64
SSM_GROUPS = 4
SSM_HEADS_PER_GROUP = 8
SSM_STATE = 128
SSM_BC = 2 * SSM_GROUPS * SSM_STATE
SSM_CONV_DIM = D_MODEL + SSM_BC
LRU_BLOCKS = 16
LRU_BLOCK_WIDTH = 128
LRU_C = 8.0
N_EXPERT_GROUPS = 4
EXPERTS_PER_GROUP = 8
N_EXPERTS = 32
TOP_K = 2
EXPERT_FF = 512
PLE_DIM = 256
LANES = 128
SUBLANES = 8

SEC_XS, SEC_Z, SEC_GY, SEC_XL, SEC_GM, SEC_GR = range(6)
PROJ_BC_OFF = 6 * D_MODEL
PROJ_DT_OFF = PROJ_BC_OFF + SSM_BC
PROJ_W = PROJ_DT_OFF + LANES
PROJ_TN = 1920

SSD_CHUNK = 128
MOE_ROWS = 512
VMEM_LIMIT = 56 * 1024 * 1024


def _cparams(sem):
    return pltpu.CompilerParams(dimension_semantics=sem, vmem_limit_bytes=VMEM_LIMIT)


def _rms(x, g):
    return x * lax.rsqrt(jnp.mean(x * x, axis=-1, keepdims=True) + RMS_EPS) * g


def _softplus(x):
    return jnp.maximum(x, 0.0) + jnp.log1p(jnp.exp(-jnp.abs(x)))


def _silu(x):
    return x * jax.nn.sigmoid(x)


def _gelu_tanh(x):
    c = 0.7978845608028654
    return x * (0.5 + 0.5 * jnp.tanh(x * (c + (c * 0.044715) * (x * x))))


def _split3(v):
    hi = v.astype(BF16)
    r1 = v - hi.astype(F32)
    mid = r1.astype(BF16)
    lo = (r1 - mid.astype(F32)).astype(BF16)
    return hi, mid, lo


def _dot(a, b):
    return jnp.dot(a, b, preferred_element_type=F32)


def _dot_exact_rhs(v, m_bf16):
    hi, mid, lo = _split3(v)
    return _dot(hi, m_bf16) + _dot(mid, m_bf16) + _dot(lo, m_bf16)


def _dot_exact_lhs(m_bf16, v):
    hi, mid, lo = _split3(v)
    return _dot(m_bf16, hi) + _dot(m_bf16, mid) + _dot(m_bf16, lo)


def _proj_kernel(x_ref, g_ref, w_ref, o_ref):
    xn = _rms(x_ref[...], g_ref[...]).astype(BF16)
    o_ref[...] = _dot(xn, w_ref[...])


def _proj(x2d, g_mix, w_in_r, tm):
    t = x2d.shape[0]
    return pl.pallas_call(
        _proj_kernel,
        grid=(PROJ_W // PROJ_TN, t // tm),
        in_specs=[
            pl.BlockSpec((tm, D_MODEL), lambda j, i: (i, 0)),
            pl.BlockSpec((1, D_MODEL), lambda j, i: (0, 0)),
            pl.BlockSpec((D_MODEL, PROJ_TN), lambda j, i: (0, j)),
        ],
        out_specs=pl.BlockSpec((tm, PROJ_TN), lambda j, i: (i, j)),
        out_shape=jax.ShapeDtypeStruct((t, PROJ_W), F32),
        compiler_params=_cparams(("arbitrary", "arbitrary")),
        name="proj",
    )(x2d, g_mix, w_in_r)


def _conv_tile(cbuf, lt, c0, c1, cw_ref, cb_ref):
    acc = cb_ref[:, c0:c1]
    for k in range(CONV_WIDTH):
        acc = acc + cbuf[SUBLANES - (CONV_WIDTH - 1) + k:SUBLANES - (CONV_WIDTH - 1) + k + lt, c0:c1] * cw_ref[k:k + 1, c0:c1]
    return acc


def _ssd_kernel(xs_ref, bc_ref, dt_ref, z_ref, s0_ref, c0_ref, cw_ref, cb_ref, dtb_ref, alog_ref, dskip_ref,
                gssm_ref, rexp_ref, tril_ref,
                y_ref, snew_ref, cnew_ref,
                cbuf, xact, bcact, yscr, state, *, lt):
    q = SSD_CHUNK
    lp = max(lt, q)
    l = pl.program_id(1)

    @pl.when(l == 0)
    def _():
        cbuf[0:SUBLANES, :] = c0_ref[...]
        state[...] = s0_ref[...].T

    cbuf[SUBLANES:SUBLANES + lt, 0:D_MODEL] = xs_ref[...]
    cbuf[SUBLANES:SUBLANES + lt, D_MODEL:SSM_CONV_DIM] = bc_ref[...]
    cw = 512
    for c0 in range(0, D_MODEL, cw):
        xact[:, c0:c0 + cw] = _silu(_conv_tile(cbuf, lt, c0, c0 + cw, cw_ref, cb_ref))
    for c0 in range(0, SSM_BC, cw):
        bcact[:, c0:c0 + cw] = _silu(_conv_tile(cbuf, lt, D_MODEL + c0, D_MODEL + c0 + cw, cw_ref, cb_ref))
    tail = cbuf[lt:lt + SUBLANES, :]
    cbuf[0:SUBLANES, :] = tail
    cnew_ref[...] = tail

    dt = _softplus(dt_ref[...] + dtb_ref[...])
    a_neg = -jnp.exp(alog_ref[...])
    da = dt * a_neg
    rexp = rexp_ref[...]
    dt_rep = _dot_exact_rhs(dt, rexp)
    tril = tril_ref[...]
    row_i = lax.broadcasted_iota(jnp.int32, (q, q), 0)
    col_i = lax.broadcasted_iota(jnp.int32, (q, q), 1)
    causal = row_i >= col_i
    lane_i = lax.broadcasted_iota(jnp.int32, (2 * q, LANES), 1)
    row2_i = lax.broadcasted_iota(jnp.int32, (2 * q, LANES), 0)
    pair_mask = (lane_i < SSM_HEAD_DIM) == (row2_i < q)

    def pad_rows(v):
        if lt == lp:
            return v
        return jnp.concatenate([v, jnp.zeros((lp - lt, v.shape[1]), v.dtype)], axis=0)

    for c in range(lp // q):
        r0 = c * q
        r1 = min(r0 + q, lt)
        da_c = pad_rows(da[r0:r1])
        acum = _dot_exact_lhs(tril, da_c)
        acum_t = acum.T
        acum_rep = _dot_exact_rhs(acum, rexp)
        last = acum_rep[q - 1:q, :]
        exp_a = jnp.exp(acum_rep)
        dec_out = jnp.exp(last - acum_rep)
        chunk_dec = jnp.exp(last)
        xa = pad_rows(xact[r0:r1, :])
        xdt = xa * pad_rows(dt_rep[r0:r1])
        xw = (xdt * dec_out).astype(BF16)
        xdt_b = xdt.astype(BF16)
        bca = pad_rows(bcact[r0:r1, :])
        for g in range(SSM_GROUPS):
            gw = SSM_HEADS_PER_GROUP * SSM_HEAD_DIM
            bg = bca[:, g * SSM_STATE:(g + 1) * SSM_STATE]
            cg = bca[:, SSM_GROUPS * SSM_STATE + g * SSM_STATE:SSM_GROUPS * SSM_STATE + (g + 1) * SSM_STATE]
            bg_b = bg.astype(BF16)
            cg_b = cg.astype(BF16)
            cb = lax.dot_general(cg_b, bg_b, (((1,), (1,)), ((), ())), preferred_element_type=F32)
            sg = state[:, g * gw:(g + 1) * gw]
            y_off = _dot(cg_b, sg.astype(BF16)) * exp_a[:, g * gw:(g + 1) * gw]
            for j in range(SSM_HEADS_PER_GROUP // 2):
                h0 = g * SSM_HEADS_PER_GROUP + 2 * j
                ms = []
                for h in (h0, h0 + 1):
                    seg = acum[:, h:h + 1] - acum_t[h:h + 1, :]
                    ms.append(cb * jnp.exp(jnp.where(causal, seg, -jnp.inf)))
                m = jnp.concatenate(ms, axis=1).astype(BF16)
                xp = xdt_b[:, h0 * SSM_HEAD_DIM:h0 * SSM_HEAD_DIM + LANES]
                rhs = jnp.where(pair_mask, jnp.concatenate([xp, xp], axis=0), jnp.zeros_like(xp[:1, :1]))
                yd = _dot(m, rhs)
                co = h0 * SSM_HEAD_DIM
                yv = yd + y_off[:, j * LANES:(j + 1) * LANES] + dskip_ref[:, co:co + LANES] * xa[:, co:co + LANES]
                yscr[r0:r1, co:co + LANES] = yv[0:r1 - r0]
            state[:, g * gw:(g + 1) * gw] = sg * chunk_dec[:, g * gw:(g + 1) * gw] + _dot(
                bg.T.astype(BF16), xw[:, g * gw:(g + 1) * gw])

    yz = yscr[...] * _silu(z_ref[...])
    y_ref[...] = _rms(yz, gssm_ref[...]).astype(BF16)

    @pl.when(l == pl.num_programs(1) - 1)
    def _():
        snew_ref[...] = state[...].T


def _ssd(proj3, s0, c0p, cw8, cb, dtb, alog, dskip_rep, gssm, rexp, tril, lt):
    b, L, _ = proj3.shape
    nl = L // lt
    kern = functools.partial(_ssd_kernel, lt=lt)
    const = lambda bi, li: (0, 0)
    return pl.pallas_call(
        kern,
        grid=(b, nl),
        in_specs=[
            pl.BlockSpec((None, lt, D_MODEL), lambda bi, li: (bi, li, SEC_XS)),
            pl.BlockSpec((None, lt, SSM_BC), lambda bi, li: (bi, li, PROJ_BC_OFF // SSM_BC)),
            pl.BlockSpec((None, lt, LANES), lambda bi, li: (bi, li, PROJ_DT_OFF // LANES)),
            pl.BlockSpec((None, lt, D_MODEL), lambda bi, li: (bi, li, SEC_Z)),
            pl.BlockSpec((None, D_MODEL, SSM_STATE), lambda bi, li: (bi, 0, 0)),
            pl.BlockSpec((None, SUBLANES, SSM_CONV_DIM), lambda bi, li: (bi, 0, 0)),
            pl.BlockSpec((SUBLANES, SSM_CONV_DIM), const),
            pl.BlockSpec((1, SSM_CONV_DIM), const),
            pl.BlockSpec((1, LANES), const),
            pl.BlockSpec((1, LANES), const),
            pl.BlockSpec((1, D_MODEL), const),
            pl.BlockSpec((1, D_MODEL), const),
            pl.BlockSpec((LANES, D_MODEL), const),
            pl.BlockSpec((SSD_CHUNK, SSD_CHUNK), const),
        ],
        out_specs=[
            pl.BlockSpec((None, lt, D_MODEL), lambda bi, li: (bi, li, 0)),
            pl.BlockSpec((None, D_MODEL, SSM_STATE), lambda bi, li: (bi, 0, 0)),
            pl.BlockSpec((None, SUBLANES, SSM_CONV_DIM), lambda bi, li: (bi, 0, 0)),
        ],
        out_shape=[
            jax.ShapeDtypeStruct((b, L, D_MODEL), BF16),
            jax.ShapeDtypeStruct((b, D_MODEL, SSM_STATE), F32),
            jax.ShapeDtypeStruct((b, SUBLANES, SSM_CONV_DIM), F32),
        ],
        scratch_shapes=[
            pltpu.VMEM((lt + SUBLANES, SSM_CONV_DIM), F32),
            pltpu.VMEM((lt, D_MODEL), F32),
            pltpu.VMEM((lt, SSM_BC), F32),
            pltpu.VMEM((lt, D_MODEL), F32),
            pltpu.VMEM((SSM_STATE, D_MODEL), F32),
        ],
        compiler_params=_cparams(("arbitrary", "arbitrary")),
        name="ssd",
    )(proj3, proj3, proj3, proj3, s0, c0p, cw8, cb, dtb, alog, dskip_rep, gssm, rexp, tril)


def _scan_rows(a, u, h_in):
    nv = a.shape[0] // SUBLANES
    a3 = a.reshape(nv, SUBLANES, LANES)
    u3 = u.reshape(nv, SUBLANES, LANES)
    sub = lax.broadcasted_iota(jnp.int32, a3.shape, 1)
    d = 1
    while d < SUBLANES:
        keep = sub >= d
        a_sh = jnp.where(keep, pltpu.roll(a3, d, 1), 1.0)
        u_sh = jnp.where(keep, pltpu.roll(u3, d, 1), 0.0)
        u3 = a3 * u_sh + u3
        a3 = a3 * a_sh
        d *= 2
    hs = []
    h = h_in
    for v in range(nv):
        hv = u3[v] + a3[v] * h
        hs.append(hv)
        h = hv[SUBLANES - 1:SUBLANES, :]
    return jnp.concatenate(hs, axis=0), h


def _lru_kernel(xl_ref, gy_ref, h0_ref, c0_ref, cw_ref, cb_ref, wax_ref, ba_ref, bx_ref, lam_ref,
                hg_ref, hnew_ref, cnew_ref,
                cbuf, hcar, *, lt):
    l = pl.program_id(1)

    @pl.when(l == 0)
    def _():
        cbuf[0:SUBLANES, :] = c0_ref[...]
        hcar[...] = h0_ref[...]

    cbuf[SUBLANES:SUBLANES + lt, :] = xl_ref[...]
    for n in range(LRU_BLOCKS):
        c0 = n * LRU_BLOCK_WIDTH
        c1 = c0 + LRU_BLOCK_WIDTH
        xc = _conv_tile(cbuf, lt, c0, c1, cw_ref, cb_ref)
        pre = _dot(xc.astype(BF16), wax_ref[n])
        r = jax.nn.sigmoid(pre[:, 0:LRU_BLOCK_WIDTH] + ba_ref[:, c0:c1])
        i = jax.nn.sigmoid(pre[:, LRU_BLOCK_WIDTH:2 * LRU_BLOCK_WIDTH] + bx_ref[:, c0:c1])
        log_a = r * ((-LRU_C) * _softplus(-lam_ref[:, c0:c1]))
        a = jnp.exp(log_a)
        t = jnp.tanh(-log_a) * (a * a + 1.0)
        u = jnp.where(t > 0.0, t * lax.rsqrt(t), 0.0) * (i * xc)
        h, h_last = _scan_rows(a, u, hcar[:, c0:c1])
        hg_ref[:, c0:c1] = (h * _gelu_tanh(gy_ref[:, c0:c1])).astype(BF16)
        hcar[:, c0:c1] = h_last
        hnew_ref[:, c0:c1] = h_last
    tail = cbuf[lt:lt + SUBLANES, :]
    cbuf[0:SUBLANES, :] = tail
    cnew_ref[...] = tail


def _lru(proj3, h0, c0p, cw8, cb, wax, ba, bx, lam, lt):
    b, L, _ = proj3.shape
    nl = L // lt
    kern = functools.partial(_lru_kernel, lt=lt)
    const = lambda bi, li: (0, 0)
    return pl.pallas_call(
        kern,
        grid=(b, nl),
        in_specs=[
            pl.BlockSpec((None, lt, D_MODEL), lambda bi, li: (bi, li, SEC_XL)),
            pl.BlockSpec((None, lt, D_MODEL), lambda bi, li: (bi, li, SEC_GY)),
            pl.BlockSpec((None, 1, D_MODEL), lambda bi, li: (bi, 0, 0)),
            pl.BlockSpec((None, SUBLANES, D_MODEL), lambda bi, li: (bi, 0, 0)),
            pl.BlockSpec((SUBLANES, D_MODEL), const),
            pl.BlockSpec((1, D_MODEL), const),
            pl.BlockSpec((LRU_BLOCKS, LRU_BLOCK_WIDTH, 2 * LRU_BLOCK_WIDTH), lambda bi, li: (0, 0, 0)),
            pl.BlockSpec((1, D_MODEL), const),
            pl.BlockSpec((1, D_MODEL), const),
            pl.BlockSpec((1, D_MODEL), const),
        ],
        out_specs=[
            pl.BlockSpec((None, lt, D_MODEL), lambda bi, li: (bi, li, 0)),
            pl.BlockSpec((None, 1, D_MODEL), lambda bi, li: (bi, 0, 0)),
            pl.BlockSpec((None, SUBLANES, D_MODEL), lambda bi, li: (bi, 0, 0)),
        ],
        out_shape=[
            jax.ShapeDtypeStruct((b, L, D_MODEL), BF16),
            jax.ShapeDtypeStruct((b, 1, D_MODEL), F32),
            jax.ShapeDtypeStruct((b, SUBLANES, D_MODEL), F32),
        ],
        scratch_shapes=[
            pltpu.VMEM((lt + SUBLANES, D_MODEL), F32),
            pltpu.VMEM((1, D_MODEL), F32),
        ],
        compiler_params=_cparams(("arbitrary", "arbitrary")),
        name="lru",
    )(proj3, proj3, h0, c0p, cw8, cb, wax, ba, bx, lam)


def _route(logits):
    neg = -1e30
    lane = lax.broadcasted_iota(jnp.int32, logits.shape, 1).astype(F32)
    is_g = lane < N_EXPERT_GROUPS
    gl = jnp.where(is_g, logits, neg)
    gmax = jnp.max(gl, axis=1, keepdims=True)
    gsum = jnp.sum(jnp.where(is_g, jnp.exp(gl - gmax), 0.0), axis=1, keepdims=True)
    g_w = 1.0 / gsum
    gidx = jnp.min(jnp.where(gl == gmax, lane, float(LANES)), axis=1, keepdims=True)
    lo = N_EXPERT_GROUPS + EXPERTS_PER_GROUP * gidx
    el = jnp.where((lane >= lo) & (lane < lo + EXPERTS_PER_GROUP), logits, neg)
    m1 = jnp.max(el, axis=1, keepdims=True)
    i1 = jnp.min(jnp.where(el == m1, lane, float(LANES)), axis=1, keepdims=True)
    el2 = jnp.where(lane == i1, neg, el)
    m2 = jnp.max(el2, axis=1, keepdims=True)
    i2 = jnp.min(jnp.where(el2 == m2, lane, float(LANES)), axis=1, keepdims=True)
    e2 = jnp.exp(m2 - m1)
    w1 = g_w / (1.0 + e2)
    w2 = g_w * e2 / (1.0 + e2)
    out = jnp.where(lane == 0, i1 - N_EXPERT_GROUPS, 0.0)
    out = jnp.where(lane == 1, i2 - N_EXPERT_GROUPS, out)
    out = jnp.where(lane == 2, w1, out)
    out = jnp.where(lane == 3, w2, out)
    return out


def _merge_kernel(x_ref, y_ref, hg_ref, gm_ref, gr_ref, wom_ref, wor_ref, wo_ref, gffn_ref, wr_ref, br_ref,
                  *rest, n_real):
    x1_ref, xn2_ref, route_ref = rest[-3:]
    i = pl.program_id(0)

    @pl.when(i < n_real)
    def _():
        o_m = _dot(y_ref[...], wom_ref[...])
        o_r = _dot(hg_ref[...], wor_ref[...])
        mixed = jax.nn.sigmoid(gm_ref[...]) * o_m + jax.nn.sigmoid(gr_ref[...]) * o_r
        x1 = x_ref[...] + _dot(mixed.astype(BF16), wo_ref[...])
        x1_ref[...] = x1
        xn2 = _rms(x1, gffn_ref[...])
        xn2_ref[...] = xn2
        x_hi = xn2.astype(BF16)
        x_lo = (xn2 - x_hi.astype(F32)).astype(BF16)
        logits = _dot(x_hi, wr_ref[0]) + _dot(x_lo, wr_ref[0]) + _dot(x_hi, wr_ref[1]) + br_ref[...]
        route_ref[...] = _route(logits)

    @pl.when(i >= n_real)
    def _():
        xn2_ref[...] = jnp.zeros_like(xn2_ref)
        route_ref[...] = jnp.zeros_like(route_ref)


def _merge(x2d, y2d, hg2d, proj2d, wom, wor, wo, gffn, wr, br, tm, t_all, row_off, shared):
    t = x2d.shape[0]
    n_real = t // tm
    off = row_off // tm
    n_extra = 0
    if shared is None and t_all > t:
        assert row_off == 0 and t_all - t <= tm
        n_extra = 1
    const = lambda i: (0, 0)
    cl = lambda i: jnp.minimum(i, n_real - 1)
    wspec = pl.BlockSpec((D_MODEL, D_MODEL), const, pipeline_mode=pl.Buffered(1))
    in_specs = [
        pl.BlockSpec((tm, D_MODEL), lambda i: (cl(i), 0)),
        pl.BlockSpec((tm, D_MODEL), lambda i: (cl(i), 0)),
        pl.BlockSpec((tm, D_MODEL), lambda i: (cl(i), 0)),
        pl.BlockSpec((tm, D_MODEL), lambda i: (cl(i), SEC_GM)),
        pl.BlockSpec((tm, D_MODEL), lambda i: (cl(i), SEC_GR)),
        wspec, wspec, wspec,
        pl.BlockSpec((1, D_MODEL), const),
        pl.BlockSpec((2, D_MODEL, LANES), lambda i: (0, 0, 0)),
        pl.BlockSpec((1, LANES), const),
    ]
    args = [x2d, y2d, hg2d, proj2d, proj2d, wom, wor, wo, gffn, wr, br]
    aliases = {}
    if shared is not None:
        in_specs += [pl.BlockSpec(memory_space=pl.ANY), pl.BlockSpec(memory_space=pl.ANY)]
        aliases = {len(args): 1, len(args) + 1: 2}
        args += list(shared)
    return pl.pallas_call(
        functools.partial(_merge_kernel, n_real=n_real),
        grid=(n_real + n_extra,),
        in_specs=in_specs,
        out_specs=[
            pl.BlockSpec((tm, D_MODEL), lambda i: (cl(i), 0)),
            pl.BlockSpec((tm, D_MODEL), lambda i: (i + off, 0)),
            pl.BlockSpec((tm, LANES), lambda i: (i + off, 0)),
        ],
        out_shape=[
            jax.ShapeDtypeStruct((t, D_MODEL), F32),
            jax.ShapeDtypeStruct((t_all, D_MODEL), F32),
            jax.ShapeDtypeStruct((t_all, LANES), F32),
        ],
        input_output_aliases=aliases,
        compiler_params=_cparams(("arbitrary",)),
        name="merge",
    )(*args)


def _moe_kernel(be_ref, nused_ref, tok_cur_ref, tok_nxt_ref, xn2_hbm, wg_ref, wu_ref, wd_ref, ys_ref, xbuf, sem):
    i = pl.program_id(0)
    n_used = nused_ref[0]

    def start_row(tok_ref, slot, r):
        pltpu.make_async_copy(xn2_hbm.at[pl.ds(tok_ref[0, r], 1), :], xbuf.at[slot, pl.ds(r, 1), :],
                              sem.at[slot]).start()

    def compute(slot, prefetch):
        pltpu.make_async_copy(xn2_hbm.at[pl.ds(0, MOE_ROWS), :], xbuf.at[slot], sem.at[slot]).wait()
        if prefetch:
            for r in range(MOE_ROWS):
                start_row(tok_nxt_ref, 1 - slot, r)
        xb = xbuf[slot].astype(BF16)
        hdn = _silu(_dot(xb, wg_ref[...])) * _dot(xb, wu_ref[...])
        ys_ref[...] = _dot(hdn.astype(BF16), wd_ref[...])

    @pl.when(jnp.logical_and(i == 0, n_used > 0))
    def _():
        def issue(r, c):
            start_row(tok_cur_ref, 0, r)
            return c
        lax.fori_loop(0, MOE_ROWS, issue, 0)

    for slot in (0, 1):
        mine = i % 2 == slot

        @pl.when(jnp.logical_and(mine, i + 1 < n_used))
        def _():
            compute(slot, True)

        @pl.when(jnp.logical_and(mine, jnp.logical_and(i < n_used, i + 1 >= n_used)))
        def _():
            compute(slot, False)

    @pl.when(i >= n_used)
    def _():
        ys_ref[...] = jnp.zeros_like(ys_ref)


def _moe(block_expert, n_used, row_token3, xn2_all, wg, wu, wd):
    n_blocks = row_token3.shape[0]
    grid_spec = pltpu.PrefetchScalarGridSpec(
        num_scalar_prefetch=2,
        grid=(n_blocks,),
        in_specs=[
            pl.BlockSpec((None, 1, MOE_ROWS), lambda i, be, nu: (i, 0, 0), memory_space=pltpu.SMEM),
            pl.BlockSpec((None, 1, MOE_ROWS), lambda i, be, nu: (jnp.minimum(i + 1, n_blocks - 1), 0, 0),
                         memory_space=pltpu.SMEM),
            pl.BlockSpec(memory_space=pl.ANY),
            pl.BlockSpec((None, D_MODEL, EXPERT_FF), lambda i, be, nu: (be[i], 0, 0)),
            pl.BlockSpec((None, D_MODEL, EXPERT_FF), lambda i, be, nu: (be[i], 0, 0)),
            pl.BlockSpec((None, EXPERT_FF, D_MODEL), lambda i, be, nu: (be[i], 0, 0)),
        ],
        out_specs=pl.BlockSpec((MOE_ROWS, D_MODEL), lambda i, be, nu: (i, 0)),
        scratch_shapes=[
            pltpu.VMEM((2, MOE_ROWS, D_MODEL), F32),
            pltpu.SemaphoreType.DMA((2,)),
        ],
    )
    return pl.pallas_call(
        _moe_kernel,
        grid_spec=grid_spec,
        out_shape=jax.ShapeDtypeStruct((n_blocks * MOE_ROWS, D_MODEL), F32),
        compiler_params=_cparams(("arbitrary",)),
        name="moe",
    )(block_expert, n_used, row_token3, row_token3, xn2_all, wg, wu, wd)


def _final_kernel(d_cur_ref, d_nxt_ref, ys_hbm, x1_ref, route_ref, p_ref, gple_ref, wpg_ref, wple_ref, gfin_ref,
                  o_ref, gbuf, sem, *, tm):
    i = pl.program_id(0)
    n = pl.num_programs(0)

    def start_row(d_ref, slot, r):
        pltpu.make_async_copy(ys_hbm.at[pl.ds(d_ref[0, r], 1), :], gbuf.at[slot, pl.ds(r, 1), :],
                              sem.at[slot]).start()

    def compute(slot, prefetch):
        pltpu.make_async_copy(ys_hbm.at[pl.ds(0, TOP_K * tm), :], gbuf.at[slot], sem.at[slot]).wait()
        if prefetch:
            for r in range(TOP_K * tm):
                start_row(d_nxt_ref, 1 - slot, r)
        route = route_ref[...]
        moe = route[:, 2:3] * gbuf[slot, 0:tm, :] + route[:, 3:4] * gbuf[slot, tm:2 * tm, :]
        x2 = x1_ref[...] + moe
        gate = jax.nn.sigmoid(_dot(_rms(x2, gple_ref[...]).astype(BF16), wpg_ref[...]))
        x3 = x2 + gate * _dot(p_ref[...].astype(BF16), wple_ref[...])
        o_ref[...] = _rms(x3, gfin_ref[...])

    @pl.when(i == 0)
    def _():
        def issue(r, c):
            start_row(d_cur_ref, 0, r)
            return c
        lax.fori_loop(0, TOP_K * tm, issue, 0)

    for slot in (0, 1):
        mine = i % 2 == slot

        @pl.when(jnp.logical_and(mine, i + 1 < n))
        def _():
            compute(slot, True)

        @pl.when(jnp.logical_and(mine, i + 1 >= n))
        def _():
            compute(slot, False)


def _final(dest3, ys, x1, route_all, p2d, gple, wpg, wple, gfin, tm, row_off):
    t = x1.shape[0]
    n = t // tm
    off = row_off // tm
    const = lambda i: (0, 0)
    kern = functools.partial(_final_kernel, tm=tm)
    return pl.pallas_call(
        kern,
        grid=(n,),
        in_specs=[
            pl.BlockSpec((None, 1, TOP_K * tm), lambda i: (i, 0, 0), memory_space=pltpu.SMEM),
            pl.BlockSpec((None, 1, TOP_K * tm), lambda i: (jnp.minimum(i + 1, n - 1), 0, 0), memory_space=pltpu.SMEM),
            pl.BlockSpec(memory_space=pl.ANY),
            pl.BlockSpec((tm, D_MODEL), lambda i: (i, 0)),
            pl.BlockSpec((tm, LANES), lambda i: (i + off, 0)),
            pl.BlockSpec((tm, PLE_DIM), lambda i: (i, 0)),
            pl.BlockSpec((1, D_MODEL), const),
            pl.BlockSpec((D_MODEL, D_MODEL), const, pipeline_mode=pl.Buffered(1)),
            pl.BlockSpec((PLE_DIM, D_MODEL), const),
            pl.BlockSpec((1, D_MODEL), const),
        ],
        out_specs=pl.BlockSpec((tm, D_MODEL), lambda i: (i, 0)),
        out_shape=jax.ShapeDtypeStruct((t, D_MODEL), F32),
        scratch_shapes=[
            pltpu.VMEM((2, TOP_K * tm, D_MODEL), F32),
            pltpu.SemaphoreType.DMA((2,)),
        ],
        compiler_params=_cparams(("arbitrary",)),
        name="final",
    )(dest3, dest3, ys, x1, route_all, p2d, gple, wpg, wple, gfin)


def _pad_rows8(a, at_end=False):
    r = a.shape[-2]
    pad = [(0, 0)] * (a.ndim - 2) + ([(0, SUBLANES - r)] if at_end else [(SUBLANES - r, 0)]) + [(0, 0)]
    return jnp.pad(a, pad)


def _dispatch(route_all):
    t_all = route_all.shape[0]
    n_assign = t_all * TOP_K
    n_blocks = -(-n_assign // MOE_ROWS) + N_EXPERTS
    expert = route_all[:, 0:TOP_K].astype(jnp.int32).reshape(-1)
    onehot = (expert[:, None] == jnp.arange(N_EXPERTS, dtype=jnp.int32)[None, :]).astype(jnp.int32)
    csum = jnp.cumsum(onehot, axis=0)
    rank = jnp.sum(csum * onehot, axis=1) - 1
    counts = csum[-1]
    nblk = (counts + MOE_ROWS - 1) // MOE_ROWS
    blk_end = jnp.cumsum(nblk)
    blk_start = blk_end - nblk
    dest = blk_start[expert] * MOE_ROWS + rank
    token = jnp.arange(n_assign, dtype=jnp.int32) // TOP_K
    row_token = jnp.zeros((n_blocks * MOE_ROWS,), jnp.int32).at[dest].set(token)
    block_expert = jnp.minimum(jnp.searchsorted(blk_end, jnp.arange(n_blocks, dtype=jnp.int32), side='right'),
                               N_EXPERTS - 1).astype(jnp.int32)
    n_used = blk_end[-1:].astype(jnp.int32)
    return dest.reshape(t_all, TOP_K).astype(jnp.int32), row_token.reshape(n_blocks, 1, MOE_ROWS), block_expert, n_used


def _tile_dest(dest, tm):
    t = dest.shape[0]
    return dest.reshape(t // tm, tm, TOP_K).transpose(0, 2, 1).reshape(t // tm, 1, TOP_K * tm)


def kernel(x_prompt, x_sample, state_ssm, state_ssm_conv, state_lru, state_lru_conv, p_prompt, p_sample, g_mix, w_in, conv_m_w, conv_m_b, dt_bias, a_log, d_skip, g_ssm, w_out_m, conv_r_w, conv_r_b, w_lru_a, b_lru_a, w_lru_x, b_lru_x, lru_lambda, w_out_r, w_o, g_ffn, w_router_group, b_router_group, w_router_expert, b_router_expert, w_exp_gate, w_exp_up, w_exp_down, g_ple, w_ple_gate, w_ple, g_final):
    depth = w_in.shape[0]
    assert depth == 1, "one layer per step"
    bp, lp_, _ = x_prompt.shape
    bs, ls, _ = x_sample.shape
    tp, ts = bp * lp_, bs * ls
    t_all = tp + ts

    w = w_in[0]
    z_end, xbc_end, dt_end = D_MODEL, D_MODEL + SSM_CONV_DIM, D_MODEL + SSM_CONV_DIM + SSM_HEADS
    off_xl = dt_end + D_MODEL
    off_gm = off_xl + D_MODEL
    off_gr = off_gm + D_MODEL
    w_in_r = jnp.concatenate([
        w[:, z_end:z_end + D_MODEL],
        w[:, 0:z_end],
        w[:, dt_end:off_xl],
        w[:, off_xl:off_gm],
        w[:, off_gm:off_gr],
        w[:, off_gr:off_gr + D_MODEL],
        w[:, z_end + D_MODEL:xbc_end],
        w[:, xbc_end:dt_end],
        jnp.zeros((D_MODEL, LANES - SSM_HEADS), w.dtype),
    ], axis=1).astype(BF16)
    row = lambda v: v.reshape(1, -1).astype(F32)
    pad_lanes = lambda v: jnp.pad(v.reshape(1, -1).astype(F32), ((0, 0), (0, LANES - v.shape[-1])))
    cw_m = _pad_rows8(conv_m_w[0].astype(F32), at_end=True)
    cw_r = _pad_rows8(conv_r_w[0].astype(F32), at_end=True)
    dskip_rep = jnp.repeat(d_skip[0].astype(F32), SSM_HEAD_DIM).reshape(1, D_MODEL)
    rexp = (jnp.arange(D_MODEL)[None, :] // SSM_HEAD_DIM == jnp.arange(LANES)[:, None]).astype(BF16)
    tril = jnp.tril(jnp.ones((SSD_CHUNK, SSD_CHUNK), F32)).astype(BF16)
    wax = jnp.concatenate([w_lru_a[0], w_lru_x[0]], axis=-1).astype(BF16)
    wom, wor, wo = w_out_m[0].astype(BF16), w_out_r[0].astype(BF16), w_o[0].astype(BF16)
    wr32 = jnp.pad(jnp.concatenate([w_router_group[0], w_router_expert[0]], axis=1).astype(F32),
                   ((0, 0), (0, LANES - N_EXPERT_GROUPS - N_EXPERTS)))
    wr_hi = wr32.astype(BF16)
    wr = jnp.stack([wr_hi, (wr32 - wr_hi.astype(F32)).astype(BF16)])
    br = pad_lanes(jnp.concatenate([b_router_group[0], b_router_expert[0]]))
    wg, wu, wd = w_exp_gate[0].astype(BF16), w_exp_up[0].astype(BF16), w_exp_down[0].astype(BF16)
    wpg, wple = w_ple_gate[0].astype(BF16), w_ple[0].astype(BF16)

    def branches(x, ssm0, sconv0, lru0, lconv0, tm_proj, lt):
        b, L, _ = x.shape
        x2d = x.reshape(b * L, D_MODEL)
        proj = _proj(x2d, row(g_mix[0]), w_in_r, tm_proj)
        proj3 = proj.reshape(b, L, PROJ_W)
        y, ssm_new, sconv8 = _ssd(proj3, ssm0.reshape(b, D_MODEL, SSM_STATE).astype(F32), _pad_rows8(sconv0.astype(F32)),
                                  cw_m, row(conv_m_b[0]), pad_lanes(dt_bias[0]), pad_lanes(a_log[0]), dskip_rep,
                                  row(g_ssm[0]), rexp, tril, lt)
        hg, lru_new, lconv8 = _lru(proj3, lru0.reshape(b, 1, D_MODEL).astype(F32), _pad_rows8(lconv0.astype(F32)),
                                   cw_r, row(conv_r_b[0]), wax, row(b_lru_a[0]), row(b_lru_x[0]), row(lru_lambda[0]), lt)
        states = (ssm_new.reshape(1, b, SSM_HEADS, SSM_HEAD_DIM, SSM_STATE),
                  sconv8[None, :, SUBLANES - (CONV_WIDTH - 1):, :],
                  lru_new.reshape(1, b, D_MODEL),
                  lconv8[None, :, SUBLANES - (CONV_WIDTH - 1):, :])
        return x2d, proj, y.reshape(b * L, D_MODEL), hg.reshape(b * L, D_MODEL), states

    zeros = lambda *s: jnp.zeros(s, F32)
    lt_p = min(256, lp_)
    tm_p = min(256, tp)
    tm_s = min(128, ts)
    xp2, proj_p, y_p, hg_p, st_p = branches(
        x_prompt, zeros(bp, SSM_HEADS, SSM_HEAD_DIM, SSM_STATE), zeros(bp, CONV_WIDTH - 1, SSM_CONV_DIM),
        zeros(bp, D_MODEL), zeros(bp, CONV_WIDTH - 1, D_MODEL), min(512, tp), lt_p)
    xs2, proj_s, y_s, hg_s, st_s = branches(
        x_sample, state_ssm[0], state_ssm_conv[0], state_lru[0], state_lru_conv[0], tm_s, ls)

    x1_p, xn2_all, route_all = _merge(xp2, y_p, hg_p, proj_p, wom, wor, wo, row(g_ffn[0]), wr, br, tm_p, t_all, 0, None)
    x1_s, xn2_all, route_all = _merge(xs2, y_s, hg_s, proj_s, wom, wor, wo, row(g_ffn[0]), wr, br, tm_s, t_all, tp,
                                      (xn2_all, route_all))

    dest, row_token3, block_expert, n_used = _dispatch(route_all)
    ys = _moe(block_expert, n_used, row_token3, xn2_all, wg, wu, wd)

    fin = lambda x1, d, p, tm, off: _final(_tile_dest(d, tm), ys, x1, route_all, p.reshape(-1, PLE_DIM).astype(F32),
                                           row(g_ple[0]), wpg, wple, row(g_final), tm, off)
    y_prompt = fin(x1_p, dest[:tp], p_prompt[0], tm_p, 0).reshape(x_prompt.shape)
    y_sample = fin(x1_s, dest[tp:], p_sample[0], tm_s, tp).reshape(x_sample.shape)
    return (y_prompt, y_sample) + st_p + st_s
```

```python
import functools

import jax
import jax.numpy as jnp
from jax import lax
from jax.experimental import pallas as pl
from jax.experimental.pallas import tpu as pltpu

F32 = jnp.float32
BF16 = jnp.bfloat16

D_MODEL = 2048
CONV_WIDTH = 4
RMS_EPS = 1e-6
SSM_HEADS = 32
SSM_HEAD_DIM = 64
SSM_GROUPS = 4
SSM_HEADS_PER_GROUP = 8
SSM_STATE = 128
SSM_BC = 2 * SSM_GROUPS * SSM_STATE
SSM_CONV_DIM = D_MODEL + SSM_BC
LRU_BLOCKS = 16
LRU_BLOCK_WIDTH = 128
LRU_C = 8.0
N_EXPERT_GROUPS = 4
EXPERTS_PER_GROUP = 8
N_EXPERTS = 32
TOP_K = 2
EXPERT_FF = 512
PLE_DIM = 256
LANES = 128
SUBLANES = 8

OFF_DT = D_MODEL + SSM_CONV_DIM
OFF_GY = OFF_DT + SSM_HEADS
IN_DIM = OFF_GY + 4 * D_MODEL
PROJ_A_W = 5376
PROJ_A_TN = 2688
PROJ_B_W = 4 * D_MODEL
PROJ_B_TN = D_MODEL
A_SEC_Z, A_SEC_XS = 0, 1
A_BLK_BC = (2 * D_MODEL) // SSM_BC
A_BLK_DT = OFF_DT // LANES
B_SEC_GY, B_SEC_XL, B_SEC_GM, B_SEC_GR = range(4)

SSD_CHUNK = 128
MOE_ROWS = 512
VMEM_LIMIT = 56 * 1024 * 1024


def _cparams(sem):
    return pltpu.CompilerParams(dimension_semantics=sem, vmem_limit_bytes=VMEM_LIMIT)


def _rms(x, g):
    return x * lax.rsqrt(jnp.mean(x * x, axis=-1, keepdims=True) + RMS_EPS) * g


def _softplus(x):
    return jnp.maximum(x, 0.0) + jnp.log1p(jnp.exp(-jnp.abs(x)))


def _silu(x):
    return x * jax.nn.sigmoid(x)


def _gelu_tanh(x):
    c = 0.7978845608028654
    return x * (0.5 + 0.5 * jnp.tanh(x * (c + (c * 0.044715) * (x * x))))


def _split3(v):
    hi = v.astype(BF16)
    r1 = v - hi.astype(F32)
    mid = r1.astype(BF16)
    lo = (r1 - mid.astype(F32)).astype(BF16)
    return hi, mid, lo


def _dot(a, b):
    return jnp.dot(a, b, preferred_element_type=F32)


def _dot_exact_rhs(v, m_bf16):
    hi, mid, lo = _split3(v)
    return _dot(hi, m_bf16) + _dot(mid, m_bf16) + _dot(lo, m_bf16)


def _dot_exact_lhs(m_bf16, v):
    hi, mid, lo = _split3(v)
    return _dot(m_bf16, hi) + _dot(m_bf16, mid) + _dot(m_bf16, lo)


def _proj_kernel(x_ref, g_ref, w_ref, o_ref):
    xn = _rms(x_ref[...], g_ref[...]).astype(BF16)
    o_ref[...] = _dot(xn, w_ref[...])


def _proj(x2d, g_mix, w, tm, tn):
    t = x2d.shape[0]
    n = w.shape[1]
    return pl.pallas_call(
        _proj_kernel,
        grid=(n // tn, t // tm),
        in_specs=[
            pl.BlockSpec((tm, D_MODEL), lambda j, i: (i, 0)),
            pl.BlockSpec((1, D_MODEL), lambda j, i: (0, 0)),
            pl.BlockSpec((D_MODEL, tn), lambda j, i: (0, j)),
        ],
        out_specs=pl.BlockSpec((tm, tn), lambda j, i: (i, j)),
        out_shape=jax.ShapeDtypeStruct((t, n), F32),
        compiler_params=_cparams(("arbitrary", "arbitrary")),
        name="proj",
    )(x2d, g_mix, w)


def _conv_tile(cbuf, lt, c0, c1, cw_ref, cb_ref):
    acc = cb_ref[:, c0:c1]
    for k in range(CONV_WIDTH):
        acc = acc + cbuf[SUBLANES - (CONV_WIDTH - 1) + k:SUBLANES - (CONV_WIDTH - 1) + k + lt, c0:c1] * cw_ref[k:k + 1, c0:c1]
    return acc


def _ssd_kernel(xs_ref, bc_ref, dt_ref, z_ref, s0_ref, c0_ref, cw_ref, cb_ref, dtb_ref, alog_ref, dskip_ref,
                gssm_ref, rexp_ref, tril_ref,
                y_ref, snew_ref, cnew_ref,
                cbuf, xact, bcact, yscr, state, *, lt):
    q = SSD_CHUNK
    lp = max(lt, q)
    l = pl.program_id(1)

    @pl.when(l == 0)
    def _():
        cbuf[0:SUBLANES, :] = c0_ref[...]
        state[...] = s0_ref[...].T

    cbuf[SUBLANES:SUBLANES + lt, 0:D_MODEL] = xs_ref[...]
    cbuf[SUBLANES:SUBLANES + lt, D_MODEL:SSM_CONV_DIM] = bc_ref[...]
    cw = 512
    for c0 in range(0, D_MODEL, cw):
        xact[:, c0:c0 + cw] = _silu(_conv_tile(cbuf, lt, c0, c0 + cw, cw_ref, cb_ref))
    for c0 in range(0, SSM_BC, cw):
        bcact[:, c0:c0 + cw] = _silu(_conv_tile(cbuf, lt, D_MODEL + c0, D_MODEL + c0 + cw, cw_ref, cb_ref))
    tail = cbuf[lt:lt + SUBLANES, :]
    cbuf[0:SUBLANES, :] = tail
    cnew_ref[...] = tail

    dt = _softplus(dt_ref[...] + dtb_ref[...])
    a_neg = -jnp.exp(alog_ref[...])
    da = dt * a_neg
    rexp = rexp_ref[...]
    dt_rep = _dot_exact_rhs(dt, rexp)
    tril = tril_ref[...]
    row_i = lax.broadcasted_iota(jnp.int32, (q, q), 0)
    col_i = lax.broadcasted_iota(jnp.int32, (q, q), 1)
    causal = row_i >= col_i
    lane_i = lax.broadcasted_iota(jnp.int32, (2 * q, LANES), 1)
    row2_i = lax.broadcasted_iota(jnp.int32, (2 * q, LANES), 0)
    pair_mask = (lane_i < SSM_HEAD_DIM) == (row2_i < q)

    def pad_rows(v):
        if lt == lp:
            return v
        return jnp.concatenate([v, jnp.zeros((lp - lt, v.shape[1]), v.dtype)], axis=0)

    for c in range(lp // q):
        r0 = c * q
        r1 = min(r0 + q, lt)
        da_c = pad_rows(da[r0:r1])
        acum = _dot_exact_lhs(tril, da_c)
        acum_t = acum.T
        acum_rep = _dot_exact_rhs(acum, rexp)
        last = acum_rep[q - 1:q, :]
        exp_a = jnp.exp(acum_rep)
        dec_out = jnp.exp(last - acum_rep)
        chunk_dec = jnp.exp(last)
        xa = pad_rows(xact[r0:r1, :])
        xdt = xa * pad_rows(dt_rep[r0:r1])
        xw = (xdt * dec_out).astype(BF16)
        xdt_b = xdt.astype(BF16)
        bca = pad_rows(bcact[r0:r1, :])
        for g in range(SSM_GROUPS):
            gw = SSM_HEADS_PER_GROUP * SSM_HEAD_DIM
            bg = bca[:, g * SSM_STATE:(g + 1) * SSM_STATE]
            cg = bca[:, SSM_GROUPS * SSM_STATE + g * SSM_STATE:SSM_GROUPS * SSM_STATE + (g + 1) * SSM_STATE]
            bg_b = bg.astype(BF16)
            cg_b = cg.astype(BF16)
            cb = lax.dot_general(cg_b, bg_b, (((1,), (1,)), ((), ())), preferred_element_type=F32)
            sg = state[:, g * gw:(g + 1) * gw]
            y_off = _dot(cg_b, sg.astype(BF16)) * exp_a[:, g * gw:(g + 1) * gw]
            for j in range(SSM_HEADS_PER_GROUP // 2):
                h0 = g * SSM_HEADS_PER_GROUP + 2 * j
                ms = []
                for h in (h0, h0 + 1):
                    seg = acum[:, h:h + 1] - acum_t[h:h + 1, :]
                    ms.append(cb * jnp.exp(jnp.where(causal, seg, -jnp.inf)))
                m = jnp.concatenate(ms, axis=1).astype(BF16)
                xp = xdt_b[:, h0 * SSM_HEAD_DIM:h0 * SSM_HEAD_DIM + LANES]
                rhs = jnp.where(pair_mask, jnp.concatenate([xp, xp], axis=0), jnp.zeros_like(xp[:1, :1]))
                yd = _dot(m, rhs)
                co = h0 * SSM_HEAD_DIM
                yv = yd + y_off[:, j * LANES:(j + 1) * LANES] + dskip_ref[:, co:co + LANES] * xa[:, co:co + LANES]
                yscr[r0:r1, co:co + LANES] = yv[0:r1 - r0]
            state[:, g * gw:(g + 1) * gw] = sg * chunk_dec[:, g * gw:(g + 1) * gw] + _dot(
                bg.T.astype(BF16), xw[:, g * gw:(g + 1) * gw])

    yz = yscr[...] * _silu(z_ref[...])
    y_ref[...] = _rms(yz, gssm_ref[...]).astype(BF16)

    @pl.when(l == pl.num_programs(1) - 1)
    def _():
        snew_ref[...] = state[...].T


def _ssd(proj3, s0, c0p, cw8, cb, dtb, alog, dskip_rep, gssm, rexp, tril, lt):
    b, L, _ = proj3.shape
    nl = L // lt
    kern = functools.partial(_ssd_kernel, lt=lt)
    const = lambda bi, li: (0, 0)
    return pl.pallas_call(
        kern,
        grid=(b, nl),
        in_specs=[
            pl.BlockSpec((None, lt, D_MODEL), lambda bi, li: (bi, li, A_SEC_XS)),
            pl.BlockSpec((None, lt, SSM_BC), lambda bi, li: (bi, li, A_BLK_BC)),
            pl.BlockSpec((None, lt, LANES), lambda bi, li: (bi, li, A_BLK_DT)),
            pl.BlockSpec((None, lt, D_MODEL), lambda bi, li: (bi, li, A_SEC_Z)),
            pl.BlockSpec((None, D_MODEL, SSM_STATE), lambda bi, li: (bi, 0, 0)),
            pl.BlockSpec((None, SUBLANES, SSM_CONV_DIM), lambda bi, li: (bi, 0, 0)),
            pl.BlockSpec((SUBLANES, SSM_CONV_DIM), const),
            pl.BlockSpec((1, SSM_CONV_DIM), const),
            pl.BlockSpec((1, LANES), const),
            pl.BlockSpec((1, LANES), const),
            pl.BlockSpec((1, D_MODEL), const),
            pl.BlockSpec((1, D_MODEL), const),
            pl.BlockSpec((LANES, D_MODEL), const),
            pl.BlockSpec((SSD_CHUNK, SSD_CHUNK), const),
        ],
        out_specs=[
            pl.BlockSpec((None, lt, D_MODEL), lambda bi, li: (bi, li, 0)),
            pl.BlockSpec((None, D_MODEL, SSM_STATE), lambda bi, li: (bi, 0, 0)),
            pl.BlockSpec((None, SUBLANES, SSM_CONV_DIM), lambda bi, li: (bi, 0, 0)),
        ],
        out_shape=[
            jax.ShapeDtypeStruct((b, L, D_MODEL), BF16),
            jax.ShapeDtypeStruct((b, D_MODEL, SSM_STATE), F32),
            jax.ShapeDtypeStruct((b, SUBLANES, SSM_CONV_DIM), F32),
        ],
        scratch_shapes=[
            pltpu.VMEM((lt + SUBLANES, SSM_CONV_DIM), F32),
            pltpu.VMEM((lt, D_MODEL), F32),
            pltpu.VMEM((lt, SSM_BC), F32),
            pltpu.VMEM((lt, D_MODEL), F32),
            pltpu.VMEM((SSM_STATE, D_MODEL), F32),
        ],
        compiler_params=_cparams(("arbitrary", "arbitrary")),
        name="ssd",
    )(proj3, proj3, proj3, proj3, s0, c0p, cw8, cb, dtb, alog, dskip_rep, gssm, rexp, tril)


def _scan_rows(a, u, h_in):
    nv = a.shape[0] // SUBLANES
    a3 = a.reshape(nv, SUBLANES, LANES)
    u3 = u.reshape(nv, SUBLANES, LANES)
    sub = lax.broadcasted_iota(jnp.int32, a3.shape, 1)
    d = 1
    while d < SUBLANES:
        keep = sub >= d
        a_sh = jnp.where(keep, pltpu.roll(a3, d, 1), 1.0)
        u_sh = jnp.where(keep, pltpu.roll(u3, d, 1), 0.0)
        u3 = a3 * u_sh + u3
        a3 = a3 * a_sh
        d *= 2
    hs = []
    h = h_in
    for v in range(nv):
        hv = u3[v] + a3[v] * h
        hs.append(hv)
        h = hv[SUBLANES - 1:SUBLANES, :]
    return jnp.concatenate(hs, axis=0), h


def _lru_kernel(xl_ref, gy_ref, h0_ref, c0_ref, cw_ref, cb_ref, wax_ref, ba_ref, bx_ref, lam_ref,
                hg_ref, hnew_ref, cnew_ref,
                cbuf, hcar, *, lt):
    l = pl.program_id(1)

    @pl.when(l == 0)
    def _():
        cbuf[0:SUBLANES, :] = c0_ref[...]
        hcar[...] = h0_ref[...]

    cbuf[SUBLANES:SUBLANES + lt, :] = xl_ref[...]
    for n in range(LRU_BLOCKS):
        c0 = n * LRU_BLOCK_WIDTH
        c1 = c0 + LRU_BLOCK_WIDTH
        xc = _conv_tile(cbuf, lt, c0, c1, cw_ref, cb_ref)
        pre = _dot(xc.astype(BF16), wax_ref[n])
        r = jax.nn.sigmoid(pre[:, 0:LRU_BLOCK_WIDTH] + ba_ref[:, c0:c1])
        i = jax.nn.sigmoid(pre[:, LRU_BLOCK_WIDTH:2 * LRU_BLOCK_WIDTH] + bx_ref[:, c0:c1])
        log_a = r * ((-LRU_C) * _softplus(-lam_ref[:, c0:c1]))
        a = jnp.exp(log_a)
        t = jnp.tanh(-log_a) * (a * a + 1.0)
        u = jnp.where(t > 0.0, t * lax.rsqrt(t), 0.0) * (i * xc)
        h, h_last = _scan_rows(a, u, hcar[:, c0:c1])
        hg_ref[:, c0:c1] = (h * _gelu_tanh(gy_ref[:, c0:c1])).astype(BF16)
        hcar[:, c0:c1] = h_last
        hnew_ref[:, c0:c1] = h_last
    tail = cbuf[lt:lt + SUBLANES, :]
    cbuf[0:SUBLANES, :] = tail
    cnew_ref[...] = tail


def _lru(proj3, h0, c0p, cw8, cb, wax, ba, bx, lam, lt):
    b, L, _ = proj3.shape
    nl = L // lt
    kern = functools.partial(_lru_kernel, lt=lt)
    const = lambda bi, li: (0, 0)
    return pl.pallas_call(
        kern,
        grid=(b, nl),
        in_specs=[
            pl.BlockSpec((None, lt, D_MODEL), lambda bi, li: (bi, li, B_SEC_XL)),
            pl.BlockSpec((None, lt, D_MODEL), lambda bi, li: (bi, li, B_SEC_GY)),
            pl.BlockSpec((None, 1, D_MODEL), lambda bi, li: (bi, 0, 0)),
            pl.BlockSpec((None, SUBLANES, D_MODEL), lambda bi, li: (bi, 0, 0)),
            pl.BlockSpec((SUBLANES, D_MODEL), const),
            pl.BlockSpec((1, D_MODEL), const),
            pl.BlockSpec((LRU_BLOCKS, LRU_BLOCK_WIDTH, 2 * LRU_BLOCK_WIDTH), lambda bi, li: (0, 0, 0)),
            pl.BlockSpec((1, D_MODEL), const),
            pl.BlockSpec((1, D_MODEL), const),
            pl.BlockSpec((1, D_MODEL), const),
        ],
        out_specs=[
            pl.BlockSpec((None, lt, D_MODEL), lambda bi, li: (bi, li, 0)),
            pl.BlockSpec((None, 1, D_MODEL), lambda bi, li: (bi, 0, 0)),
            pl.BlockSpec((None, SUBLANES, D_MODEL), lambda bi, li: (bi, 0, 0)),
        ],
        out_shape=[
            jax.ShapeDtypeStruct((b, L, D_MODEL), BF16),
            jax.ShapeDtypeStruct((b, 1, D_MODEL), F32),
            jax.ShapeDtypeStruct((b, SUBLANES, D_MODEL), F32),
        ],
        scratch_shapes=[
            pltpu.VMEM((lt + SUBLANES, D_MODEL), F32),
            pltpu.VMEM((1, D_MODEL), F32),
        ],
        compiler_params=_cparams(("arbitrary", "arbitrary")),
        name="lru",
    )(proj3, proj3, h0, c0p, cw8, cb, wax, ba, bx, lam)


def _route(logits):
    neg = -1e30
    lane = lax.broadcasted_iota(jnp.int32, logits.shape, 1).astype(F32)
    is_g = lane < N_EXPERT_GROUPS
    gl = jnp.where(is_g, logits, neg)
    gmax = jnp.max(gl, axis=1, keepdims=True)
    gsum = jnp.sum(jnp.where(is_g, jnp.exp(gl - gmax), 0.0), axis=1, keepdims=True)
    g_w = 1.0 / gsum
    gidx = jnp.min(jnp.where(gl == gmax, lane, float(LANES)), axis=1, keepdims=True)
    lo = N_EXPERT_GROUPS + EXPERTS_PER_GROUP * gidx
    el = jnp.where((lane >= lo) & (lane < lo + EXPERTS_PER_GROUP), logits, neg)
    m1 = jnp.max(el, axis=1, keepdims=True)
    i1 = jnp.min(jnp.where(el == m1, lane, float(LANES)), axis=1, keepdims=True)
    el2 = jnp.where(lane == i1, neg, el)
    m2 = jnp.max(el2, axis=1, keepdims=True)
    i2 = jnp.min(jnp.where(el2 == m2, lane, float(LANES)), axis=1, keepdims=True)
    e2 = jnp.exp(m2 - m1)
    w1 = g_w / (1.0 + e2)
    w2 = g_w * e2 / (1.0 + e2)
    out = jnp.where(lane == 0, i1 - N_EXPERT_GROUPS, 0.0)
    out = jnp.where(lane == 1, i2 - N_EXPERT_GROUPS, out)
    out = jnp.where(lane == 2, w1, out)
    out = jnp.where(lane == 3, w2, out)
    return out


def _merge_kernel(x_ref, y_ref, hg_ref, gm_ref, gr_ref, wom_ref, wor_ref, wo_ref, gffn_ref, wr_ref, br_ref,
                  *rest, n_real):
    x1_ref, xn2_ref, route_ref = rest[-3:]
    i = pl.program_id(0)

    @pl.when(i < n_real)
    def _():
        o_m = _dot(y_ref[...], wom_ref[...])
        o_r = _dot(hg_ref[...], wor_ref[...])
        mixed = jax.nn.sigmoid(gm_ref[...]) * o_m + jax.nn.sigmoid(gr_ref[...]) * o_r
        x1 = x_ref[...] + _dot(mixed.astype(BF16), wo_ref[...])
        x1_ref[...] = x1
        xn2 = _rms(x1, gffn_ref[...])
        xn2_ref[...] = xn2
        x_hi = xn2.astype(BF16)
        x_lo = (xn2 - x_hi.astype(F32)).astype(BF16)
        logits = _dot(x_hi, wr_ref[0]) + _dot(x_lo, wr_ref[0]) + _dot(x_hi, wr_ref[1]) + br_ref[...]
        route_ref[...] = _route(logits)

    @pl.when(i >= n_real)
    def _():
        xn2_ref[...] = jnp.zeros_like(xn2_ref)
        route_ref[...] = jnp.zeros_like(route_ref)


def _merge(x2d, y2d, hg2d, proj2d, wom, wor, wo, gffn, wr, br, tm, t_all, row_off, shared):
    t = x2d.shape[0]
    n_real = t // tm
    off = row_off // tm
    n_extra = 0
    if shared is None and t_all > t:
        assert row_off == 0 and t_all - t <= tm
        n_extra = 1
    const = lambda i: (0, 0)
    cl = lambda i: jnp.minimum(i, n_real - 1)
    wspec = pl.BlockSpec((D_MODEL, D_MODEL), const, pipeline_mode=pl.Buffered(1))
    in_specs = [
        pl.BlockSpec((tm, D_MODEL), lambda i: (cl(i), 0)),
        pl.BlockSpec((tm, D_MODEL), lambda i: (cl(i), 0)),
        pl.BlockSpec((tm, D_MODEL), lambda i: (cl(i), 0)),
        pl.BlockSpec((tm, D_MODEL), lambda i: (cl(i), B_SEC_GM)),
        pl.BlockSpec((tm, D_MODEL), lambda i: (cl(i), B_SEC_GR)),
        wspec, wspec, wspec,
        pl.BlockSpec((1, D_MODEL), const),
        pl.BlockSpec((2, D_MODEL, LANES), lambda i: (0, 0, 0)),
        pl.BlockSpec((1, LANES), const),
    ]
    args = [x2d, y2d, hg2d, proj2d, proj2d, wom, wor, wo, gffn, wr, br]
    aliases = {}
    if shared is not None:
        in_specs += [pl.BlockSpec(memory_space=pl.ANY), pl.BlockSpec(memory_space=pl.ANY)]
        aliases = {len(args): 1, len(args) + 1: 2}
        args += list(shared)
    return pl.pallas_call(
        functools.partial(_merge_kernel, n_real=n_real),
        grid=(n_real + n_extra,),
        in_specs=in_specs,
        out_specs=[
            pl.BlockSpec((tm, D_MODEL), lambda i: (cl(i), 0)),
            pl.BlockSpec((tm, D_MODEL), lambda i: (i + off, 0)),
            pl.BlockSpec((tm, LANES), lambda i: (i + off, 0)),
        ],
        out_shape=[
            jax.ShapeDtypeStruct((t, D_MODEL), F32),
            jax.ShapeDtypeStruct((t_all, D_MODEL), F32),
            jax.ShapeDtypeStruct((t_all, LANES), F32),
        ],
        input_output_aliases=aliases,
        compiler_params=_cparams(("arbitrary",)),
        name="merge",
    )(*args)


def _moe_kernel(be_ref, nused_ref, tok_cur_ref, tok_nxt_ref, xn2_hbm, wg_ref, wu_ref, wd_ref, ys_ref,
                xbuf0, xbuf1, wgb, wub, wdb, sem):
    i = pl.program_id(0)
    n_used = nused_ref[0]
    xbufs = (xbuf0, xbuf1)

    def start_row(tok_ref, slot, r):
        pltpu.make_async_copy(xn2_hbm.at[pl.ds(tok_ref[0, r], 1), :], xbufs[slot].at[pl.ds(r, 1), :],
                              sem.at[slot]).start()

    def compute(slot, prefetch):
        pltpu.make_async_copy(xn2_hbm.at[pl.ds(0, MOE_ROWS), :], xbufs[slot], sem.at[slot]).wait()
        xb = xbufs[slot][...].astype(BF16)
        third = MOE_ROWS // 3

        def prefetch_rows(r0, r1):
            if prefetch:
                for r in range(r0, r1):
                    start_row(tok_nxt_ref, 1 - slot, r)

        prefetch_rows(0, third)
        hg = _dot(xb, wgb[...])
        prefetch_rows(third, 2 * third)
        hu = _dot(xb, wub[...])
        prefetch_rows(2 * third, MOE_ROWS)
        ys_ref[...] = _dot((_silu(hg) * hu).astype(BF16), wdb[...])

    fresh = jnp.logical_or(i == 0, be_ref[i] != be_ref[jnp.maximum(i - 1, 0)])

    @pl.when(jnp.logical_and(fresh, i < n_used))
    def _():
        wgb[...] = wg_ref[...].astype(BF16)
        wub[...] = wu_ref[...].astype(BF16)
        wdb[...] = wd_ref[...].astype(BF16)

    @pl.when(jnp.logical_and(i == 0, n_used > 0))
    def _():
        def issue(r, c):
            start_row(tok_cur_ref, 0, r)
            return c
        lax.fori_loop(0, MOE_ROWS, issue, 0)

    for slot in (0, 1):
        mine = i % 2 == slot

        @pl.when(jnp.logical_and(mine, i + 1 < n_used))
        def _():
            compute(slot, True)

        @pl.when(jnp.logical_and(mine, jnp.logical_and(i < n_used, i + 1 >= n_used)))
        def _():
            compute(slot, False)

    @pl.when(i >= n_used)
    def _():
        ys_ref[...] = jnp.zeros_like(ys_ref)


def _moe(block_expert, n_used, row_token3, xn2_all, wg, wu, wd):
    n_blocks = row_token3.shape[0]
    grid_spec = pltpu.PrefetchScalarGridSpec(
        num_scalar_prefetch=2,
        grid=(n_blocks,),
        in_specs=[
            pl.BlockSpec((None, 1, MOE_ROWS), lambda i, be, nu: (i, 0, 0), memory_space=pltpu.SMEM),
            pl.BlockSpec((None, 1, MOE_ROWS), lambda i, be, nu: (jnp.minimum(i + 1, n_blocks - 1), 0, 0),
                         memory_space=pltpu.SMEM),
            pl.BlockSpec(memory_space=pl.ANY),
            pl.BlockSpec((None, D_MODEL, EXPERT_FF), lambda i, be, nu: (be[i], 0, 0)),
            pl.BlockSpec((None, D_MODEL, EXPERT_FF), lambda i, be, nu: (be[i], 0, 0)),
            pl.BlockSpec((None, EXPERT_FF, D_MODEL), lambda i, be, nu: (be[i], 0, 0)),
        ],
        out_specs=pl.BlockSpec((MOE_ROWS, D_MODEL), lambda i, be, nu: (i, 0)),
        scratch_shapes=[
            pltpu.VMEM((MOE_ROWS, D_MODEL), F32),
            pltpu.VMEM((MOE_ROWS, D_MODEL), F32),
            pltpu.VMEM((D_MODEL, EXPERT_FF), BF16),
            pltpu.VMEM((D_MODEL, EXPERT_FF), BF16),
            pltpu.VMEM((EXPERT_FF, D_MODEL), BF16),
            pltpu.SemaphoreType.DMA((2,)),
        ],
    )
    return pl.pallas_call(
        _moe_kernel,
        grid_spec=grid_spec,
        out_shape=jax.ShapeDtypeStruct((n_blocks * MOE_ROWS, D_MODEL), F32),
        compiler_params=_cparams(("arbitrary",)),
        name="moe",
    )(block_expert, n_used, row_token3, row_token3, xn2_all, wg, wu, wd)


def _final_kernel(d_cur_ref, d_nxt_ref, ys_hbm, x1_ref, route_ref, p_ref, gple_ref, wpg_ref, wple_ref, gfin_ref,
                  o_ref, gbuf0, gbuf1, sem, *, tm):
    i = pl.program_id(0)
    n = pl.num_programs(0)
    gbufs = (gbuf0, gbuf1)

    def start_row(d_ref, slot, r):
        pltpu.make_async_copy(ys_hbm.at[pl.ds(d_ref[0, r], 1), :], gbufs[slot].at[pl.ds(r, 1), :],
                              sem.at[slot]).start()

    def compute(slot, prefetch):
        gbuf = gbufs[slot]
        pltpu.make_async_copy(ys_hbm.at[pl.ds(0, TOP_K * tm), :], gbuf, sem.at[slot]).wait()
        route = route_ref[...]
        moe = route[:, 2:3] * gbuf[0:tm, :] + route[:, 3:4] * gbuf[tm:2 * tm, :]
        x2 = x1_ref[...] + moe
        xn3 = _rms(x2, gple_ref[...]).astype(BF16)
        if prefetch:
            for r in range(TOP_K * tm):
                start_row(d_nxt_ref, 1 - slot, r)
        gate = jax.nn.sigmoid(_dot(xn3, wpg_ref[...]))
        x3 = x2 + gate * _dot(p_ref[...].astype(BF16), wple_ref[...])
        o_ref[...] = _rms(x3, gfin_ref[...])

    @pl.when(i == 0)
    def _():
        def issue(r, c):
            start_row(d_cur_ref, 0, r)
            return c
        lax.fori_loop(0, TOP_K * tm, issue, 0)

    for slot in (0, 1):
        mine = i % 2 == slot

        @pl.when(jnp.logical_and(mine, i + 1 < n))
        def _():
            compute(slot, True)

        @pl.when(jnp.logical_and(mine, i + 1 >= n))
        def _():
            compute(slot, False)


def _final(dest3, ys, x1, route_all, p2d, gple, wpg, wple, gfin, tm, row_off):
    t = x1.shape[0]
    n = t // tm
    off = row_off // tm
    const = lambda i: (0, 0)
    kern = functools.partial(_final_kernel, tm=tm)
    return pl.pallas_call(
        kern,
        grid=(n,),
        in_specs=[
            pl.BlockSpec((None, 1, TOP_K * tm), lambda i: (i, 0, 0), memory_space=pltpu.SMEM),
            pl.BlockSpec((None, 1, TOP_K * tm), lambda i: (jnp.minimum(i + 1, n - 1), 0, 0), memory_space=pltpu.SMEM),
            pl.BlockSpec(memory_space=pl.ANY),
            pl.BlockSpec((tm, D_MODEL), lambda i: (i, 0)),
            pl.BlockSpec((tm, LANES), lambda i: (i + off, 0)),
            pl.BlockSpec((tm, PLE_DIM), lambda i: (i, 0)),
            pl.BlockSpec((1, D_MODEL), const),
            pl.BlockSpec((D_MODEL, D_MODEL), const, pipeline_mode=pl.Buffered(1)),
            pl.BlockSpec((PLE_DIM, D_MODEL), const),
            pl.BlockSpec((1, D_MODEL), const),
        ],
        out_specs=pl.BlockSpec((tm, D_MODEL), lambda i: (i, 0)),
        out_shape=jax.ShapeDtypeStruct((t, D_MODEL), F32),
        scratch_shapes=[
            pltpu.VMEM((TOP_K * tm, D_MODEL), F32),
            pltpu.VMEM((TOP_K * tm, D_MODEL), F32),
            pltpu.SemaphoreType.DMA((2,)),
        ],
        compiler_params=_cparams(("arbitrary",)),
        name="final",
    )(dest3, dest3, ys, x1, route_all, p2d, gple, wpg, wple, gfin)


def _pad_rows8(a, at_end=False):
    r = a.shape[-2]
    pad = [(0, 0)] * (a.ndim - 2) + ([(0, SUBLANES - r)] if at_end else [(SUBLANES - r, 0)]) + [(0, 0)]
    return jnp.pad(a, pad)


def _dispatch(route_all):
    t_all = route_all.shape[0]
    n_assign = t_all * TOP_K
    n_blocks = -(-n_assign // MOE_ROWS) + N_EXPERTS
    expert = route_all[:, 0:TOP_K].astype(jnp.int32).reshape(-1)
    onehot = (expert[:, None] == jnp.arange(N_EXPERTS, dtype=jnp.int32)[None, :]).astype(jnp.int32)
    csum = jnp.cumsum(onehot, axis=0)
    rank = jnp.sum(csum * onehot, axis=1) - 1
    counts = csum[-1]
    nblk = (counts + MOE_ROWS - 1) // MOE_ROWS
    blk_end = jnp.cumsum(nblk)
    blk_start = blk_end - nblk
    dest = blk_start[expert] * MOE_ROWS + rank
    token = jnp.arange(n_assign, dtype=jnp.int32) // TOP_K
    row_token = jnp.zeros((n_blocks * MOE_ROWS,), jnp.int32).at[dest].set(token)
    block_expert = jnp.minimum(jnp.sum(blk_end[None, :] <= jnp.arange(n_blocks, dtype=jnp.int32)[:, None], axis=1),
                               N_EXPERTS - 1).astype(jnp.int32)
    n_used = blk_end[-1:].astype(jnp.int32)
    return dest.reshape(t_all, TOP_K).astype(jnp.int32), row_token.reshape(n_blocks, 1, MOE_ROWS), block_expert, n_used


def _tile_dest(dest, tm):
    t = dest.shape[0]
    return dest.reshape(t // tm, tm, TOP_K).transpose(0, 2, 1).reshape(t // tm, 1, TOP_K * tm)


def kernel(x_prompt, x_sample, state_ssm, state_ssm_conv, state_lru, state_lru_conv, p_prompt, p_sample, g_mix, w_in, conv_m_w, conv_m_b, dt_bias, a_log, d_skip, g_ssm, w_out_m, conv_r_w, conv_r_b, w_lru_a, b_lru_a, w_lru_x, b_lru_x, lru_lambda, w_out_r, w_o, g_ffn, w_router_group, b_router_group, w_router_expert, b_router_expert, w_exp_gate, w_exp_up, w_exp_down, g_ple, w_ple_gate, w_ple, g_final):
    depth = w_in.shape[0]
    assert depth == 1, "one layer per step"
    bp, lp_, _ = x_prompt.shape
    bs, ls, _ = x_sample.shape
    tp, ts = bp * lp_, bs * ls
    t_all = tp + ts

    w = w_in[0]
    assert w.shape == (D_MODEL, IN_DIM)
    w_a = w[:, 0:PROJ_A_W].astype(BF16)
    w_b = w[:, OFF_GY:IN_DIM].astype(BF16)
    row = lambda v: v.reshape(1, -1).astype(F32)
    pad_lanes = lambda v: jnp.pad(v.reshape(1, -1).astype(F32), ((0, 0), (0, LANES - v.shape[-1])))
    cw_m = _pad_rows8(conv_m_w[0].astype(F32), at_end=True)
    cw_r = _pad_rows8(conv_r_w[0].astype(F32), at_end=True)
    dskip_rep = jnp.repeat(d_skip[0].astype(F32), SSM_HEAD_DIM).reshape(1, D_MODEL)
    rexp = (jnp.arange(D_MODEL)[None, :] // SSM_HEAD_DIM == jnp.arange(LANES)[:, None]).astype(BF16)
    tril = jnp.tril(jnp.ones((SSD_CHUNK, SSD_CHUNK), F32)).astype(BF16)
    wax = jnp.concatenate([w_lru_a[0], w_lru_x[0]], axis=-1).astype(BF16)
    wom, wor, wo = w_out_m[0].astype(BF16), w_out_r[0].astype(BF16), w_o[0].astype(BF16)
    wr32 = jnp.pad(jnp.concatenate([w_router_group[0], w_router_expert[0]], axis=1).astype(F32),
                   ((0, 0), (0, LANES - N_EXPERT_GROUPS - N_EXPERTS)))
    wr_hi = wr32.astype(BF16)
    wr = jnp.stack([wr_hi, (wr32 - wr_hi.astype(F32)).astype(BF16)])
    br = pad_lanes(jnp.concatenate([b_router_group[0], b_router_expert[0]]))
    wg, wu, wd = w_exp_gate[0].astype(F32), w_exp_up[0].astype(F32), w_exp_down[0].astype(F32)
    wpg, wple = w_ple_gate[0].astype(BF16), w_ple[0].astype(BF16)

    def branches(x, ssm0, sconv0, lru0, lconv0, tm_proj, lt):
        b, L, _ = x.shape
        x2d = x.reshape(b * L, D_MODEL)
        proj_a = _proj(x2d, row(g_mix[0]), w_a, tm_proj, PROJ_A_TN)
        proj_b = _proj(x2d, row(g_mix[0]), w_b, tm_proj, PROJ_B_TN)
        y, ssm_new, sconv8 = _ssd(proj_a.reshape(b, L, PROJ_A_W), ssm0.reshape(b, D_MODEL, SSM_STATE).astype(F32), _pad_rows8(sconv0.astype(F32)),
                                  cw_m, row(conv_m_b[0]), pad_lanes(dt_bias[0]), pad_lanes(a_log[0]), dskip_rep,
                                  row(g_ssm[0]), rexp, tril, lt)
        hg, lru_new, lconv8 = _lru(proj_b.reshape(b, L, PROJ_B_W), lru0.reshape(b, 1, D_MODEL).astype(F32), _pad_rows8(lconv0.astype(F32)),
                                   cw_r, row(conv_r_b[0]), wax, row(b_lru_a[0]), row(b_lru_x[0]), row(lru_lambda[0]), lt)
        states = (ssm_new.reshape(1, b, SSM_HEADS, SSM_HEAD_DIM, SSM_STATE),
                  sconv8[None, :, SUBLANES - (CONV_WIDTH - 1):, :],
                  lru_new.reshape(1, b, D_MODEL),
                  lconv8[None, :, SUBLANES - (CONV_WIDTH - 1):, :])
        return x2d, proj_b, y.reshape(b * L, D_MODEL), hg.reshape(b * L, D_MODEL), states

    zeros = lambda *s: jnp.zeros(s, F32)
    lt_p = min(256, lp_)
    tm_p = min(256, tp)
    tm_s = min(128, ts)
    xp2, proj_p, y_p, hg_p, st_p = branches(
        x_prompt, zeros(bp, SSM_HEADS, SSM_HEAD_DIM, SSM_STATE), zeros(bp, CONV_WIDTH - 1, SSM_CONV_DIM),
        zeros(bp, D_MODEL), zeros(bp, CONV_WIDTH - 1, D_MODEL), min(512, tp), lt_p)
    xs2, proj_s, y_s, hg_s, st_s = branches(
        x_sample, state_ssm[0], state_ssm_conv[0], state_lru[0], state_lru_conv[0], tm_s, ls)

    x1_p, xn2_all, route_all = _merge(xp2, y_p, hg_p, proj_p, wom, wor, wo, row(g_ffn[0]), wr, br, tm_p, t_all, 0, None)
    x1_s, xn2_all, route_all = _merge(xs2, y_s, hg_s, proj_s, wom, wor, wo, row(g_ffn[0]), wr, br, tm_s, t_all, tp,
                                      (xn2_all, route_all))

    dest, row_token3, block_expert, n_used = _dispatch(route_all)
    ys = _moe(block_expert, n_used, row_token3, xn2_all, wg, wu, wd)

    fin = lambda x1, d, p, tm, off: _final(_tile_dest(d, tm), ys, x1, route_all, p.reshape(-1, PLE_DIM).astype(F32),
                                           row(g_ple[0]), wpg, wple, row(g_final), tm, off)
    y_prompt = fin(x1_p, dest[:tp], p_prompt[0], tm_p, 0).reshape(x_prompt.shape)
    y_sample = fin(x1_s, dest[tp:], p_sample[0], tm_s, tp).reshape(x_sample.shape)
    return (y_prompt, y_sample) + st_p + st_s
```

```python
import functools

import jax
import jax.numpy as jnp
from jax import lax
from jax.experimental import pallas as pl
from jax.experimental.pallas import tpu as pltpu

F32 = jnp.float32
BF16 = jnp.bfloat16

D_MODEL = 2048
CONV_WIDTH = 4
RMS_EPS = 1e-6
SSM_HEADS = 32
SSM_HEAD_DIM = 64
SSM_GROUPS = 4
SSM_HEADS_PER_GROUP = 8
SSM_STATE = 128
SSM_BC = 2 * SSM_GROUPS * SSM_STATE
SSM_CONV_DIM = D_MODEL + SSM_BC
LRU_BLOCKS = 16
LRU_BLOCK_WIDTH = 128
LRU_C = 8.0
N_EXPERT_GROUPS = 4
EXPERTS_PER_GROUP = 8
N_EXPERTS = 32
TOP_K = 2
EXPERT_FF = 512
PLE_DIM = 256
LANES = 128
SUBLANES = 8
MXU_WIDTH = 256

OFF_XS = D_MODEL
OFF_BC = 2 * D_MODEL
OFF_DT = D_MODEL + SSM_CONV_DIM
OFF_GY = OFF_DT + SSM_HEADS
IN_DIM = OFF_GY + 4 * D_MODEL
PROJ_A_TAIL = 1280
PROJ_A_W = D_MODEL + PROJ_A_TAIL
PROJ_A_TN = PROJ_A_W // 2
A_SEC_Z = 0
A_BLK_BC = D_MODEL // SSM_BC
A_BLK_DT = (D_MODEL + SSM_BC) // LANES
B_SEC_GY, B_SEC_XL, B_SEC_GM, B_SEC_GR = range(4)
G_SEC_GY, G_SEC_GM, G_SEC_GR = range(3)

SSD_CHUNK = 128
MOE_ROWS = 512
VMEM_LIMIT = 56 * 1024 * 1024


def _cparams(sem):
    return pltpu.CompilerParams(dimension_semantics=sem, vmem_limit_bytes=VMEM_LIMIT)


def _rms(x, g):
    return x * lax.rsqrt(jnp.mean(x * x, axis=-1, keepdims=True) + RMS_EPS) * g


def _softplus(x):
    return jnp.maximum(x, 0.0) + jnp.log1p(jnp.exp(-jnp.abs(x)))


def _silu(x):
    return x * jax.nn.sigmoid(x)


def _gelu_tanh(x):
    c = 0.7978845608028654
    return x * (0.5 + 0.5 * jnp.tanh(x * (c + (c * 0.044715) * (x * x))))


def _split3(v):
    hi = v.astype(BF16)
    r1 = v - hi.astype(F32)
    mid = r1.astype(BF16)
    lo = (r1 - mid.astype(F32)).astype(BF16)
    return hi, mid, lo


def _dot(a, b):
    return jnp.dot(a, b, preferred_element_type=F32)


def _dot_exact_rhs(v, m_bf16):
    hi, mid, lo = _split3(v)
    return _dot(hi, m_bf16) + _dot(mid, m_bf16) + _dot(lo, m_bf16)


def _dot_exact_lhs(m_bf16, v):
    hi, mid, lo = _split3(v)
    return _dot(m_bf16, hi) + _dot(m_bf16, mid) + _dot(m_bf16, lo)


def _proj_a_kernel(x_ref, g_ref, w_ref, o_ref, xn_ref):
    @pl.when(pl.program_id(1) == 0)
    def _():
        xn_ref[...] = _rms(x_ref[...], g_ref[...]).astype(BF16)

    o_ref[...] = _dot(xn_ref[...], w_ref[...])


def _proj_a(x2d, g_mix, w_a, tm):
    t = x2d.shape[0]
    return pl.pallas_call(
        _proj_a_kernel,
        grid=(t // tm, PROJ_A_W // PROJ_A_TN),
        in_specs=[
            pl.BlockSpec((tm, D_MODEL), lambda i, j: (i, 0)),
            pl.BlockSpec((1, D_MODEL), lambda i, j: (0, 0)),
            pl.BlockSpec((D_MODEL, PROJ_A_TN), lambda i, j: (0, j)),
        ],
        out_specs=[
            pl.BlockSpec((tm, PROJ_A_TN), lambda i, j: (i, j)),
            pl.BlockSpec((tm, D_MODEL), lambda i, j: (i, 0)),
        ],
        out_shape=[
            jax.ShapeDtypeStruct((t, PROJ_A_W), F32),
            jax.ShapeDtypeStruct((t, D_MODEL), BF16),
        ],
        compiler_params=_cparams(("arbitrary", "arbitrary")),
        name="proj_a",
    )(x2d, g_mix, w_a)


def _proj_gates_kernel(xn_ref, w_ref, o_ref):
    j = pl.program_id(0)
    xn = xn_ref[...]

    def run(act):
        for n0 in range(0, D_MODEL, MXU_WIDTH):
            o_ref[:, n0:n0 + MXU_WIDTH] = act(_dot(xn, w_ref[:, n0:n0 + MXU_WIDTH]))

    @pl.when(j == G_SEC_GY)
    def _():
        run(_gelu_tanh)

    @pl.when(j != G_SEC_GY)
    def _():
        run(jax.nn.sigmoid)


def _proj_gates(xn2d, w_b, tm):
    t = xn2d.shape[0]
    skip_xl = lambda j: jnp.where(j >= B_SEC_XL, j + 1, j)
    return pl.pallas_call(
        _proj_gates_kernel,
        grid=(3, t // tm),
        in_specs=[
            pl.BlockSpec((tm, D_MODEL), lambda j, i: (i, 0)),
            pl.BlockSpec((D_MODEL, D_MODEL), lambda j, i: (0, skip_xl(j))),
        ],
        out_specs=pl.BlockSpec((tm, D_MODEL), lambda j, i: (i, j)),
        out_shape=jax.ShapeDtypeStruct((t, 3 * D_MODEL), F32),
        compiler_params=_cparams(("arbitrary", "arbitrary")),
        name="proj_gates",
    )(xn2d, w_b)


def _conv_tile(cbuf, lt, c0, c1, cw_ref, cb_ref):
    acc = cb_ref[:, c0:c1]
    for k in range(CONV_WIDTH):
        acc = acc + cbuf[SUBLANES - (CONV_WIDTH - 1) + k:SUBLANES - (CONV_WIDTH - 1) + k + lt, c0:c1] * cw_ref[k:k + 1, c0:c1]
    return acc


def _ssd_kernel(xn_ref, wxs_ref, bc_ref, dt_ref, z_ref, s0_ref, c0_ref, cw_ref, cb_ref, dtb_ref, alog_ref, dskip_ref,
                gssm_ref, rexp_ref, tril_ref,
                y_ref, snew_ref, cnew_ref,
                cbuf, xact, bcact, yscr, state, *, lt):
    q = SSD_CHUNK
    lp = max(lt, q)
    l = pl.program_id(1)

    @pl.when(l == 0)
    def _():
        cbuf[0:SUBLANES, :] = c0_ref[...]
        state[...] = s0_ref[...].T

    cbuf[SUBLANES:SUBLANES + lt, D_MODEL:SSM_CONV_DIM] = bc_ref[...]
    cw = 512
    xn = xn_ref[...]
    for c0 in range(0, D_MODEL, cw):
        cbuf[SUBLANES:SUBLANES + lt, c0:c0 + cw] = _dot(xn, wxs_ref[:, c0:c0 + cw])
        xact[:, c0:c0 + cw] = _silu(_conv_tile(cbuf, lt, c0, c0 + cw, cw_ref, cb_ref))
    for c0 in range(0, SSM_BC, cw):
        bcact[:, c0:c0 + cw] = _silu(_conv_tile(cbuf, lt, D_MODEL + c0, D_MODEL + c0 + cw, cw_ref, cb_ref))
    tail = cbuf[lt:lt + SUBLANES, :]
    cbuf[0:SUBLANES, :] = tail
    cnew_ref[...] = tail

    dt = _softplus(dt_ref[...] + dtb_ref[...])
    a_neg = -jnp.exp(alog_ref[...])
    da = dt * a_neg
    rexp = rexp_ref[...]
    dt_rep = _dot_exact_rhs(dt, rexp)
    tril = tril_ref[...]
    row_i = lax.broadcasted_iota(jnp.int32, (q, q), 0)
    col_i = lax.broadcasted_iota(jnp.int32, (q, q), 1)
    causal = row_i >= col_i
    lane_i = lax.broadcasted_iota(jnp.int32, (2 * q, LANES), 1)
    row2_i = lax.broadcasted_iota(jnp.int32, (2 * q, LANES), 0)
    pair_mask = (lane_i < SSM_HEAD_DIM) == (row2_i < q)

    def pad_rows(v):
        if lt == lp:
            return v
        return jnp.concatenate([v, jnp.zeros((lp - lt, v.shape[1]), v.dtype)], axis=0)

    for c in range(lp // q):
        r0 = c * q
        r1 = min(r0 + q, lt)
        da_c = pad_rows(da[r0:r1])
        acum = _dot_exact_lhs(tril, da_c)
        acum_t = acum.T
        acum_rep = _dot_exact_rhs(acum, rexp)
        last = acum_rep[q - 1:q, :]
        exp_a = jnp.exp(acum_rep)
        dec_out = jnp.exp(last - acum_rep)
        chunk_dec = jnp.exp(last)
        xa = pad_rows(xact[r0:r1, :])
        xdt = xa * pad_rows(dt_rep[r0:r1])
        xw = (xdt * dec_out).astype(BF16)
        xdt_b = xdt.astype(BF16)
        bca = pad_rows(bcact[r0:r1, :])
        for g in range(SSM_GROUPS):
            gw = SSM_HEADS_PER_GROUP * SSM_HEAD_DIM
            bg = bca[:, g * SSM_STATE:(g + 1) * SSM_STATE]
            cg = bca[:, SSM_GROUPS * SSM_STATE + g * SSM_STATE:SSM_GROUPS * SSM_STATE + (g + 1) * SSM_STATE]
            bg_b = bg.astype(BF16)
            cg_b = cg.astype(BF16)
            cb = lax.dot_general(cg_b, bg_b, (((1,), (1,)), ((), ())), preferred_element_type=F32)
            sg = state[:, g * gw:(g + 1) * gw]
            y_off = _dot(cg_b, sg.astype(BF16)) * exp_a[:, g * gw:(g + 1) * gw]
            for j in range(SSM_HEADS_PER_GROUP // 2):
                h0 = g * SSM_HEADS_PER_GROUP + 2 * j
                ms = []
                for h in (h0, h0 + 1):
                    seg = acum[:, h:h + 1] - acum_t[h:h + 1, :]
                    ms.append(cb * jnp.exp(jnp.where(causal, seg, -jnp.inf)))
                m = jnp.concatenate(ms, axis=1).astype(BF16)
                xp = xdt_b[:, h0 * SSM_HEAD_DIM:h0 * SSM_HEAD_DIM + LANES]
                rhs = jnp.where(pair_mask, jnp.concatenate([xp, xp], axis=0), jnp.zeros_like(xp[:1, :1]))
                yd = _dot(m, rhs)
                co = h0 * SSM_HEAD_DIM
                yv = yd + y_off[:, j * LANES:(j + 1) * LANES] + dskip_ref[:, co:co + LANES] * xa[:, co:co + LANES]
                yscr[r0:r1, co:co + LANES] = yv[0:r1 - r0]
            state[:, g * gw:(g + 1) * gw] = sg * chunk_dec[:, g * gw:(g + 1) * gw] + _dot(
                bg.T.astype(BF16), xw[:, g * gw:(g + 1) * gw])

    yz = yscr[...] * _silu(z_ref[...])
    y_ref[...] = _rms(yz, gssm_ref[...]).astype(BF16)

    @pl.when(l == pl.num_programs(1) - 1)
    def _():
        snew_ref[...] = state[...].T


def _ssd(xn3, w_xs, proj3, s0, c0p, cw8, cb, dtb, alog, dskip_rep, gssm, rexp, tril, lt):
    b, L, _ = proj3.shape
    nl = L // lt
    kern = functools.partial(_ssd_kernel, lt=lt)
    const = lambda bi, li: (0, 0)
    return pl.pallas_call(
        kern,
        grid=(b, nl),
        in_specs=[
            pl.BlockSpec((None, lt, D_MODEL), lambda bi, li: (bi, li, 0)),
            pl.BlockSpec((D_MODEL, D_MODEL), const, pipeline_mode=pl.Buffered(1)),
            pl.BlockSpec((None, lt, SSM_BC), lambda bi, li: (bi, li, A_BLK_BC)),
            pl.BlockSpec((None, lt, LANES), lambda bi, li: (bi, li, A_BLK_DT)),
            pl.BlockSpec((None, lt, D_MODEL), lambda bi, li: (bi, li, A_SEC_Z)),
            pl.BlockSpec((None, D_MODEL, SSM_STATE), lambda bi, li: (bi, 0, 0)),
            pl.BlockSpec((None, SUBLANES, SSM_CONV_DIM), lambda bi, li: (bi, 0, 0)),
            pl.BlockSpec((SUBLANES, SSM_CONV_DIM), const),
            pl.BlockSpec((1, SSM_CONV_DIM), const),
            pl.BlockSpec((1, LANES), const),
            pl.BlockSpec((1, LANES), const),
            pl.BlockSpec((1, D_MODEL), const),
            pl.BlockSpec((1, D_MODEL), const),
            pl.BlockSpec((LANES, D_MODEL), const),
            pl.BlockSpec((SSD_CHUNK, SSD_CHUNK), const),
        ],
        out_specs=[
            pl.BlockSpec((None, lt, D_MODEL), lambda bi, li: (bi, li, 0)),
            pl.BlockSpec((None, D_MODEL, SSM_STATE), lambda bi, li: (bi, 0, 0)),
            pl.BlockSpec((None, SUBLANES, SSM_CONV_DIM), lambda bi, li: (bi, 0, 0)),
        ],
        out_shape=[
            jax.ShapeDtypeStruct((b, L, D_MODEL), BF16),
            jax.ShapeDtypeStruct((b, D_MODEL, SSM_STATE), F32),
            jax.ShapeDtypeStruct((b, SUBLANES, SSM_CONV_DIM), F32),
        ],
        scratch_shapes=[
            pltpu.VMEM((lt + SUBLANES, SSM_CONV_DIM), F32),
            pltpu.VMEM((lt, D_MODEL), F32),
            pltpu.VMEM((lt, SSM_BC), F32),
            pltpu.VMEM((lt, D_MODEL), F32),
            pltpu.VMEM((SSM_STATE, D_MODEL), F32),
        ],
        compiler_params=_cparams(("arbitrary", "arbitrary")),
        name="ssd",
    )(xn3, w_xs, proj3, proj3, proj3, s0, c0p, cw8, cb, dtb, alog, dskip_rep, gssm, rexp, tril)


def _scan_rows(a, u, h_in):
    nv = a.shape[0] // SUBLANES
    a3 = a.reshape(nv, SUBLANES, LANES)
    u3 = u.reshape(nv, SUBLANES, LANES)
    sub = lax.broadcasted_iota(jnp.int32, a3.shape, 1)
    d = 1
    while d < SUBLANES:
        keep = sub >= d
        a_sh = jnp.where(keep, pltpu.roll(a3, d, 1), 1.0)
        u_sh = jnp.where(keep, pltpu.roll(u3, d, 1), 0.0)
        u3 = a3 * u_sh + u3
        a3 = a3 * a_sh
        d *= 2
    hs = []
    h = h_in
    for v in range(nv):
        hv = u3[v] + a3[v] * h
        hs.append(hv)
        h = hv[SUBLANES - 1:SUBLANES, :]
    return jnp.concatenate(hs, axis=0), h


def _lru_kernel(xn_ref, wxl_ref, ggy_ref, h0_ref, c0_ref, cw_ref, cb_ref, wax_ref, ba_ref, bx_ref, lam_ref,
                hg_ref, hnew_ref, cnew_ref,
                cbuf, hcar, *, lt):
    l = pl.program_id(1)

    @pl.when(l == 0)
    def _():
        cbuf[0:SUBLANES, :] = c0_ref[...]
        hcar[...] = h0_ref[...]

    cbuf[SUBLANES:SUBLANES + lt, :] = _dot(xn_ref[...], wxl_ref[...])
    for n in range(LRU_BLOCKS):
        c0 = n * LRU_BLOCK_WIDTH
        c1 = c0 + LRU_BLOCK_WIDTH
        xc = _conv_tile(cbuf, lt, c0, c1, cw_ref, cb_ref)
        pre = _dot(xc.astype(BF16), wax_ref[n])
        r = jax.nn.sigmoid(pre[:, 0:LRU_BLOCK_WIDTH] + ba_ref[:, c0:c1])
        i = jax.nn.sigmoid(pre[:, LRU_BLOCK_WIDTH:2 * LRU_BLOCK_WIDTH] + bx_ref[:, c0:c1])
        log_a = r * ((-LRU_C) * _softplus(-lam_ref[:, c0:c1]))
        a = jnp.exp(log_a)
        t = jnp.tanh(-log_a) * (a * a + 1.0)
        u = jnp.where(t > 0.0, t * lax.rsqrt(t), 0.0) * (i * xc)
        h, h_last = _scan_rows(a, u, hcar[:, c0:c1])
        hg_ref[:, c0:c1] = (h * ggy_ref[:, c0:c1]).astype(BF16)
        hcar[:, c0:c1] = h_last
        hnew_ref[:, c0:c1] = h_last
    tail = cbuf[lt:lt + SUBLANES, :]
    cbuf[0:SUBLANES, :] = tail
    cnew_ref[...] = tail


def _lru(xn3, w_b, gates3, h0, c0p, cw8, cb, wax, ba, bx, lam, lt):
    b, L, _ = xn3.shape
    nl = L // lt
    kern = functools.partial(_lru_kernel, lt=lt)
    const = lambda bi, li: (0, 0)
    return pl.pallas_call(
        kern,
        grid=(b, nl),
        in_specs=[
            pl.BlockSpec((None, lt, D_MODEL), lambda bi, li: (bi, li, 0)),
            pl.BlockSpec((D_MODEL, D_MODEL), lambda bi, li: (0, B_SEC_XL), pipeline_mode=pl.Buffered(1)),
            pl.BlockSpec((None, lt, D_MODEL), lambda bi, li: (bi, li, G_SEC_GY)),
            pl.BlockSpec((None, 1, D_MODEL), lambda bi, li: (bi, 0, 0)),
            pl.BlockSpec((None, SUBLANES, D_MODEL), lambda bi, li: (bi, 0, 0)),
            pl.BlockSpec((SUBLANES, D_MODEL), const),
            pl.BlockSpec((1, D_MODEL), const),
            pl.BlockSpec((LRU_BLOCKS, LRU_BLOCK_WIDTH, 2 * LRU_BLOCK_WIDTH), lambda bi, li: (0, 0, 0)),
            pl.BlockSpec((1, D_MODEL), const),
            pl.BlockSpec((1, D_MODEL), const),
            pl.BlockSpec((1, D_MODEL), const),
        ],
        out_specs=[
            pl.BlockSpec((None, lt, D_MODEL), lambda bi, li: (bi, li, 0)),
            pl.BlockSpec((None, 1, D_MODEL), lambda bi, li: (bi, 0, 0)),
            pl.BlockSpec((None, SUBLANES, D_MODEL), lambda bi, li: (bi, 0, 0)),
        ],
        out_shape=[
            jax.ShapeDtypeStruct((b, L, D_MODEL), BF16),
            jax.ShapeDtypeStruct((b, 1, D_MODEL), F32),
            jax.ShapeDtypeStruct((b, SUBLANES, D_MODEL), F32),
        ],
        scratch_shapes=[
            pltpu.VMEM((lt + SUBLANES, D_MODEL), F32),
            pltpu.VMEM((1, D_MODEL), F32),
        ],
        compiler_params=_cparams(("arbitrary", "arbitrary")),
        name="lru",
    )(xn3, w_b, gates3, h0, c0p, cw8, cb, wax, ba, bx, lam)


def _route(logits):
    neg = -1e30
    lane = lax.broadcasted_iota(jnp.int32, logits.shape, 1).astype(F32)
    is_g = lane < N_EXPERT_GROUPS
    gl = jnp.where(is_g, logits, neg)
    gmax = jnp.max(gl, axis=1, keepdims=True)
    gsum = jnp.sum(jnp.where(is_g, jnp.exp(gl - gmax), 0.0), axis=1, keepdims=True)
    g_w = 1.0 / gsum
    gidx = jnp.min(jnp.where(gl == gmax, lane, float(LANES)), axis=1, keepdims=True)
    lo = N_EXPERT_GROUPS + EXPERTS_PER_GROUP * gidx
    el = jnp.where((lane >= lo) & (lane < lo + EXPERTS_PER_GROUP), logits, neg)
    m1 = jnp.max(el, axis=1, keepdims=True)
    i1 = jnp.min(jnp.where(el == m1, lane, float(LANES)), axis=1, keepdims=True)
    el2 = jnp.where(lane == i1, neg, el)
    m2 = jnp.max(el2, axis=1, keepdims=True)
    i2 = jnp.min(jnp.where(el2 == m2, lane, float(LANES)), axis=1, keepdims=True)
    e2 = jnp.exp(m2 - m1)
    w1 = g_w / (1.0 + e2)
    w2 = g_w * e2 / (1.0 + e2)
    out = jnp.where(lane == 0, i1 - N_EXPERT_GROUPS, 0.0)
    out = jnp.where(lane == 1, i2 - N_EXPERT_GROUPS, out)
    out = jnp.where(lane == 2, w1, out)
    out = jnp.where(lane == 3, w2, out)
    return out


def _merge_kernel(x_ref, y_ref, hg_ref, gm_ref, gr_ref, wom_ref, wor_ref, wo_ref, gffn_ref, wr_ref, br_ref,
                  *rest, n_real):
    x1_ref, xn2_ref, route_ref = rest[-3:]
    i = pl.program_id(0)

    @pl.when(i < n_real)
    def _():
        o_m = _dot(y_ref[...], wom_ref[...])
        o_r = _dot(hg_ref[...], wor_ref[...])
        mixed = gm_ref[...] * o_m + gr_ref[...] * o_r
        x1 = x_ref[...] + _dot(mixed.astype(BF16), wo_ref[...])
        x1_ref[...] = x1
        xn2 = _rms(x1, gffn_ref[...])
        xn2_ref[...] = xn2
        x_hi = xn2.astype(BF16)
        x_lo = (xn2 - x_hi.astype(F32)).astype(BF16)
        logits = _dot(x_hi, wr_ref[0]) + _dot(x_lo, wr_ref[0]) + _dot(x_hi, wr_ref[1]) + br_ref[...]
        route_ref[...] = _route(logits)

    @pl.when(i >= n_real)
    def _():
        xn2_ref[...] = jnp.zeros_like(xn2_ref)
        route_ref[...] = jnp.zeros_like(route_ref)


def _merge(x2d, y2d, hg2d, proj2d, wom, wor, wo, gffn, wr, br, tm, t_all, row_off, shared):
    t = x2d.shape[0]
    n_real = t // tm
    off = row_off // tm
    n_extra = 0
    if shared is None and t_all > t:
        assert row_off == 0 and t_all - t <= tm
        n_extra = 1
    const = lambda i: (0, 0)
    cl = lambda i: jnp.minimum(i, n_real - 1)
    wspec = pl.BlockSpec((D_MODEL, D_MODEL), const, pipeline_mode=pl.Buffered(1))
    in_specs = [
        pl.BlockSpec((tm, D_MODEL), lambda i: (cl(i), 0)),
        pl.BlockSpec((tm, D_MODEL), lambda i: (cl(i), 0)),
        pl.BlockSpec((tm, D_MODEL), lambda i: (cl(i), 0)),
        pl.BlockSpec((tm, D_MODEL), lambda i: (cl(i), G_SEC_GM)),
        pl.BlockSpec((tm, D_MODEL), lambda i: (cl(i), G_SEC_GR)),
        wspec, wspec, wspec,
        pl.BlockSpec((1, D_MODEL), const),
        pl.BlockSpec((2, D_MODEL, LANES), lambda i: (0, 0, 0)),
        pl.BlockSpec((1, LANES), const),
    ]
    args = [x2d, y2d, hg2d, proj2d, proj2d, wom, wor, wo, gffn, wr, br]
    aliases = {}
    if shared is not None:
        in_specs += [pl.BlockSpec(memory_space=pl.ANY), pl.BlockSpec(memory_space=pl.ANY)]
        aliases = {len(args): 1, len(args) + 1: 2}
        args += list(shared)
    return pl.pallas_call(
        functools.partial(_merge_kernel, n_real=n_real),
        grid=(n_real + n_extra,),
        in_specs=in_specs,
        out_specs=[
            pl.BlockSpec((tm, D_MODEL), lambda i: (cl(i), 0)),
            pl.BlockSpec((tm, D_MODEL), lambda i: (i + off, 0)),
            pl.BlockSpec((tm, LANES), lambda i: (i + off, 0)),
        ],
        out_shape=[
            jax.ShapeDtypeStruct((t, D_MODEL), F32),
            jax.ShapeDtypeStruct((t_all, D_MODEL), F32),
            jax.ShapeDtypeStruct((t_all, LANES), F32),
        ],
        input_output_aliases=aliases,
        compiler_params=_cparams(("arbitrary",)),
        name="merge",
    )(*args)


def _moe_kernel(be_ref, nused_ref, tok_cur_ref, tok_nxt_ref, xn2_hbm, wg_ref, wu_ref, wd_ref, ys_ref,
                xbuf0, xbuf1, wgb, wub, wdb, sem):
    i = pl.program_id(0)
    n_used = nused_ref[0]
    xbufs = (xbuf0, xbuf1)

    def start_row(tok_ref, slot, r):
        pltpu.make_async_copy(xn2_hbm.at[pl.ds(tok_ref[0, r], 1), :], xbufs[slot].at[pl.ds(r, 1), :],
                              sem.at[slot]).start()

    def compute(slot, prefetch):
        pltpu.make_async_copy(xn2_hbm.at[pl.ds(0, MOE_ROWS), :], xbufs[slot], sem.at[slot]).wait()
        xb = xbufs[slot][...].astype(BF16)
        third = MOE_ROWS // 3

        def prefetch_rows(r0, r1):
            if prefetch:
                for r in range(r0, r1):
                    start_row(tok_nxt_ref, 1 - slot, r)

        prefetch_rows(0, third)
        hg = _dot(xb, wgb[...])
        prefetch_rows(third, 2 * third)
        hu = _dot(xb, wub[...])
        prefetch_rows(2 * third, MOE_ROWS)
        ys_ref[...] = _dot((_silu(hg) * hu).astype(BF16), wdb[...])

    fresh = jnp.logical_or(i == 0, be_ref[i] != be_ref[jnp.maximum(i - 1, 0)])

    @pl.when(jnp.logical_and(fresh, i < n_used))
    def _():
        wgb[...] = wg_ref[...].astype(BF16)
        wub[...] = wu_ref[...].astype(BF16)
        wdb[...] = wd_ref[...].astype(BF16)

    @pl.when(jnp.logical_and(i == 0, n_used > 0))
    def _():
        def issue(r, c):
            start_row(tok_cur_ref, 0, r)
            return c
        lax.fori_loop(0, MOE_ROWS, issue, 0)

    for slot in (0, 1):
        mine = i % 2 == slot

        @pl.when(jnp.logical_and(mine, i + 1 < n_used))
        def _():
            compute(slot, True)

        @pl.when(jnp.logical_and(mine, jnp.logical_and(i < n_used, i + 1 >= n_used)))
        def _():
            compute(slot, False)

    @pl.when(i >= n_used)
    def _():
        ys_ref[...] = jnp.zeros_like(ys_ref)


def _moe(block_expert, n_used, row_token3, xn2_all, wg, wu, wd):
    n_blocks = row_token3.shape[0]
    grid_spec = pltpu.PrefetchScalarGridSpec(
        num_scalar_prefetch=2,
        grid=(n_blocks,),
        in_specs=[
            pl.BlockSpec((None, 1, MOE_ROWS), lambda i, be, nu: (i, 0, 0), memory_space=pltpu.SMEM),
            pl.BlockSpec((None, 1, MOE_ROWS), lambda i, be, nu: (jnp.minimum(i + 1, n_blocks - 1), 0, 0),
                         memory_space=pltpu.SMEM),
            pl.BlockSpec(memory_space=pl.ANY),
            pl.BlockSpec((None, D_MODEL, EXPERT_FF), lambda i, be, nu: (be[i], 0, 0)),
            pl.BlockSpec((None, D_MODEL, EXPERT_FF), lambda i, be, nu: (be[i], 0, 0)),
            pl.BlockSpec((None, EXPERT_FF, D_MODEL), lambda i, be, nu: (be[i], 0, 0)),
        ],
        out_specs=pl.BlockSpec((MOE_ROWS, D_MODEL), lambda i, be, nu: (i, 0)),
        scratch_shapes=[
            pltpu.VMEM((MOE_ROWS, D_MODEL), F32),
            pltpu.VMEM((MOE_ROWS, D_MODEL), F32),
            pltpu.VMEM((D_MODEL, EXPERT_FF), BF16),
            pltpu.VMEM((D_MODEL, EXPERT_FF), BF16),
            pltpu.VMEM((EXPERT_FF, D_MODEL), BF16),
            pltpu.SemaphoreType.DMA((2,)),
        ],
    )
    return pl.pallas_call(
        _moe_kernel,
        grid_spec=grid_spec,
        out_shape=jax.ShapeDtypeStruct((n_blocks * MOE_ROWS, D_MODEL), F32),
        compiler_params=_cparams(("arbitrary",)),
        name="moe",
    )(block_expert, n_used, row_token3, row_token3, xn2_all, wg, wu, wd)


def _final_kernel(d_cur_ref, d_nxt_ref, ys_hbm, x1_ref, route_ref, p_ref, gple_ref, wpg_ref, wple_ref, gfin_ref,
                  o_ref, gbuf0, gbuf1, sem, *, tm):
    i = pl.program_id(0)
    n = pl.num_programs(0)
    gbufs = (gbuf0, gbuf1)

    def start_row(d_ref, slot, r):
        pltpu.make_async_copy(ys_hbm.at[pl.ds(d_ref[0, r], 1), :], gbufs[slot].at[pl.ds(r, 1), :],
                              sem.at[slot]).start()

    def compute(slot, prefetch):
        gbuf = gbufs[slot]
        pltpu.make_async_copy(ys_hbm.at[pl.ds(0, TOP_K * tm), :], gbuf, sem.at[slot]).wait()
        route = route_ref[...]
        moe = route[:, 2:3] * gbuf[0:tm, :] + route[:, 3:4] * gbuf[tm:2 * tm, :]
        x2 = x1_ref[...] + moe
        xn3 = _rms(x2, gple_ref[...]).astype(BF16)
        if prefetch:
            for r in range(TOP_K * tm):
                start_row(d_nxt_ref, 1 - slot, r)
        gate = jax.nn.sigmoid(_dot(xn3, wpg_ref[...]))
        x3 = x2 + gate * _dot(p_ref[...].astype(BF16), wple_ref[...])
        o_ref[...] = _rms(x3, gfin_ref[...])

    @pl.when(i == 0)
    def _():
        def issue(r, c):
            start_row(d_cur_ref, 0, r)
            return c
        lax.fori_loop(0, TOP_K * tm, issue, 0)

    for slot in (0, 1):
        mine = i % 2 == slot

        @pl.when(jnp.logical_and(mine, i + 1 < n))
        def _():
            compute(slot, True)

        @pl.when(jnp.logical_and(mine, i + 1 >= n))
        def _():
            compute(slot, False)


def _final(dest3, ys, x1, route_all, p2d, gple, wpg, wple, gfin, tm, row_off):
    t = x1.shape[0]
    n = t // tm
    off = row_off // tm
    const = lambda i: (0, 0)
    kern = functools.partial(_final_kernel, tm=tm)
    return pl.pallas_call(
        kern,
        grid=(n,),
        in_specs=[
            pl.BlockSpec((None, 1, TOP_K * tm), lambda i: (i, 0, 0), memory_space=pltpu.SMEM),
            pl.BlockSpec((None, 1, TOP_K * tm), lambda i: (jnp.minimum(i + 1, n - 1), 0, 0), memory_space=pltpu.SMEM),
            pl.BlockSpec(memory_space=pl.ANY),
            pl.BlockSpec((tm, D_MODEL), lambda i: (i, 0)),
            pl.BlockSpec((tm, LANES), lambda i: (i + off, 0)),
            pl.BlockSpec((tm, PLE_DIM), lambda i: (i, 0)),
            pl.BlockSpec((1, D_MODEL), const),
            pl.BlockSpec((D_MODEL, D_MODEL), const, pipeline_mode=pl.Buffered(1)),
            pl.BlockSpec((PLE_DIM, D_MODEL), const),
            pl.BlockSpec((1, D_MODEL), const),
        ],
        out_specs=pl.BlockSpec((tm, D_MODEL), lambda i: (i, 0)),
        out_shape=jax.ShapeDtypeStruct((t, D_MODEL), F32),
        scratch_shapes=[
            pltpu.VMEM((TOP_K * tm, D_MODEL), F32),
            pltpu.VMEM((TOP_K * tm, D_MODEL), F32),
            pltpu.SemaphoreType.DMA((2,)),
        ],
        compiler_params=_cparams(("arbitrary",)),
        name="final",
    )(dest3, dest3, ys, x1, route_all, p2d, gple, wpg, wple, gfin)


def _pad_rows8(a, at_end=False):
    r = a.shape[-2]
    pad = [(0, 0)] * (a.ndim - 2) + ([(0, SUBLANES - r)] if at_end else [(SUBLANES - r, 0)]) + [(0, 0)]
    return jnp.pad(a, pad)


def _dispatch(route_all):
    t_all = route_all.shape[0]
    n_assign = t_all * TOP_K
    n_blocks = -(-n_assign // MOE_ROWS) + N_EXPERTS
    expert = route_all[:, 0:TOP_K].astype(jnp.int32).reshape(-1)
    onehot = (expert[:, None] == jnp.arange(N_EXPERTS, dtype=jnp.int32)[None, :]).astype(jnp.int32)
    csum = jnp.cumsum(onehot, axis=0)
    rank = jnp.sum(csum * onehot, axis=1) - 1
    counts = csum[-1]
    nblk = (counts + MOE_ROWS - 1) // MOE_ROWS
    blk_end = jnp.cumsum(nblk)
    blk_start = blk_end - nblk
    dest = blk_start[expert] * MOE_ROWS + rank
    token = jnp.arange(n_assign, dtype=jnp.int32) // TOP_K
    row_token = jnp.zeros((n_blocks * MOE_ROWS,), jnp.int32).at[dest].set(token)
    block_expert = jnp.minimum(jnp.sum(blk_end[None, :] <= jnp.arange(n_blocks, dtype=jnp.int32)[:, None], axis=1),
                               N_EXPERTS - 1).astype(jnp.int32)
    n_used = blk_end[-1:].astype(jnp.int32)
    return dest.reshape(t_all, TOP_K).astype(jnp.int32), row_token.reshape(n_blocks, 1, MOE_ROWS), block_expert, n_used


def _tile_dest(dest, tm):
    t = dest.shape[0]
    return dest.reshape(t // tm, tm, TOP_K).transpose(0, 2, 1).reshape(t // tm, 1, TOP_K * tm)


def kernel(x_prompt, x_sample, state_ssm, state_ssm_conv, state_lru, state_lru_conv, p_prompt, p_sample, g_mix, w_in, conv_m_w, conv_m_b, dt_bias, a_log, d_skip, g_ssm, w_out_m, conv_r_w, conv_r_b, w_lru_a, b_lru_a, w_lru_x, b_lru_x, lru_lambda, w_out_r, w_o, g_ffn, w_router_group, b_router_group, w_router_expert, b_router_expert, w_exp_gate, w_exp_up, w_exp_down, g_ple, w_ple_gate, w_ple, g_final):
    depth = w_in.shape[0]
    assert depth == 1, "one layer per step"
    bp, lp_, _ = x_prompt.shape
    bs, ls, _ = x_sample.shape
    tp, ts = bp * lp_, bs * ls
    t_all = tp + ts

    w = w_in[0]
    assert w.shape == (D_MODEL, IN_DIM)
    w_a = jnp.concatenate([w[:, 0:OFF_XS], w[:, OFF_BC:OFF_BC + PROJ_A_TAIL]], axis=1).astype(BF16)
    w_xs = w[:, OFF_XS:OFF_BC].astype(BF16)
    w_b = w[:, OFF_GY:IN_DIM].astype(BF16)
    row = lambda v: v.reshape(1, -1).astype(F32)
    pad_lanes = lambda v: jnp.pad(v.reshape(1, -1).astype(F32), ((0, 0), (0, LANES - v.shape[-1])))
    cw_m = _pad_rows8(conv_m_w[0].astype(F32), at_end=True)
    cw_r = _pad_rows8(conv_r_w[0].astype(F32), at_end=True)
    dskip_rep = jnp.repeat(d_skip[0].astype(F32), SSM_HEAD_DIM).reshape(1, D_MODEL)
    rexp = (jnp.arange(D_MODEL)[None, :] // SSM_HEAD_DIM == jnp.arange(LANES)[:, None]).astype(BF16)
    tril = jnp.tril(jnp.ones((SSD_CHUNK, SSD_CHUNK), F32)).astype(BF16)
    wax = jnp.concatenate([w_lru_a[0], w_lru_x[0]], axis=-1).astype(BF16)
    wom, wor, wo = w_out_m[0].astype(BF16), w_out_r[0].astype(BF16), w_o[0].astype(BF16)
    wr32 = jnp.pad(jnp.concatenate([w_router_group[0], w_router_expert[0]], axis=1).astype(F32),
                   ((0, 0), (0, LANES - N_EXPERT_GROUPS - N_EXPERTS)))
    wr_hi = wr32.astype(BF16)
    wr = jnp.stack([wr_hi, (wr32 - wr_hi.astype(F32)).astype(BF16)])
    br = pad_lanes(jnp.concatenate([b_router_group[0], b_router_expert[0]]))
    wg, wu, wd = w_exp_gate[0].astype(F32), w_exp_up[0].astype(F32), w_exp_down[0].astype(F32)
    wpg, wple = w_ple_gate[0].astype(BF16), w_ple[0].astype(BF16)

    def branches(x, ssm0, sconv0, lru0, lconv0, tm_proj, lt):
        b, L, _ = x.shape
        x2d = x.reshape(b * L, D_MODEL)
        proj_a, xn = _proj_a(x2d, row(g_mix[0]), w_a, tm_proj)
        gates = _proj_gates(xn, w_b, tm_proj)
        xn3 = xn.reshape(b, L, D_MODEL)
        y, ssm_new, sconv8 = _ssd(xn3, w_xs, proj_a.reshape(b, L, PROJ_A_W),
                                  ssm0.reshape(b, D_MODEL, SSM_STATE).astype(F32), _pad_rows8(sconv0.astype(F32)),
                                  cw_m, row(conv_m_b[0]), pad_lanes(dt_bias[0]), pad_lanes(a_log[0]), dskip_rep,
                                  row(g_ssm[0]), rexp, tril, lt)
        hg, lru_new, lconv8 = _lru(xn3, w_b, gates.reshape(b, L, 3 * D_MODEL),
                                   lru0.reshape(b, 1, D_MODEL).astype(F32), _pad_rows8(lconv0.astype(F32)),
                                   cw_r, row(conv_r_b[0]), wax, row(b_lru_a[0]), row(b_lru_x[0]), row(lru_lambda[0]), lt)
        states = (ssm_new.reshape(1, b, SSM_HEADS, SSM_HEAD_DIM, SSM_STATE),
                  sconv8[None, :, SUBLANES - (CONV_WIDTH - 1):, :],
                  lru_new.reshape(1, b, D_MODEL),
                  lconv8[None, :, SUBLANES - (CONV_WIDTH - 1):, :])
        return x2d, gates, y.reshape(b * L, D_MODEL), hg.reshape(b * L, D_MODEL), states

    zeros = lambda *s: jnp.zeros(s, F32)
    lt_p = min(256, lp_)
    tm_p = min(256, tp)
    tm_s = min(128, ts)
    xp2, proj_p, y_p, hg_p, st_p = branches(
        x_prompt, zeros(bp, SSM_HEADS, SSM_HEAD_DIM, SSM_STATE), zeros(bp, CONV_WIDTH - 1, SSM_CONV_DIM),
        zeros(bp, D_MODEL), zeros(bp, CONV_WIDTH - 1, D_MODEL), min(512, tp), lt_p)
    xs2, proj_s, y_s, hg_s, st_s = branches(
        x_sample, state_ssm[0], state_ssm_conv[0], state_lru[0], state_lru_conv[0], tm_s, ls)

    x1_p, xn2_all, route_all = _merge(xp2, y_p, hg_p, proj_p, wom, wor, wo, row(g_ffn[0]), wr, br, tm_p, t_all, 0, None)
    x1_s, xn2_all, route_all = _merge(xs2, y_s, hg_s, proj_s, wom, wor, wo, row(g_ffn[0]), wr, br, tm_s, t_all, tp,
                                      (xn2_all, route_all))

    dest, row_token3, block_expert, n_used = _dispatch(route_all)
    ys = _moe(block_expert, n_used, row_token3, xn2_all, wg, wu, wd)

    fin = lambda x1, d, p, tm, off: _final(_tile_dest(d, tm), ys, x1, route_all, p.reshape(-1, PLE_DIM).astype(F32),
                                           row(g_ple[0]), wpg, wple, row(g_final), tm, off)
    y_prompt = fin(x1_p, dest[:tp], p_prompt[0], tm_p, 0).reshape(x_prompt.shape)
    y_sample = fin(x1_s, dest[tp:], p_sample[0], tm_s, tp).reshape(x_sample.shape)
    return (y_prompt, y_sample) + st_p + st_s
```

```python
import functools

import jax
import jax.numpy as jnp
from jax import lax
from jax.experimental import pallas as pl
from jax.experimental.pallas import tpu as pltpu

F32 = jnp.float32
BF16 = jnp.bfloat16
U32 = jnp.uint32

D_MODEL = 2048
HALF_D = D_MODEL // 2
CONV_WIDTH = 4
RMS_EPS = 1e-6
SSM_HEADS = 32
SSM_HEAD_DIM = 64
SSM_GROUPS = 4
SSM_HEADS_PER_GROUP = 8
SSM_STATE = 128
SSM_BC = 2 * SSM_GROUPS * SSM_STATE
SSM_CONV_DIM = D_MODEL + SSM_BC
LRU_BLOCKS = 16
LRU_BLOCK_WIDTH = 128
LRU_C = 8.0
N_EXPERT_GROUPS = 4
EXPERTS_PER_GROUP = 8
N_EXPERTS = 32
TOP_K = 2
EXPERT_FF = 512
PLE_DIM = 256
LANES = 128
SUBLANES = 8
MXU_WIDTH = 256

OFF_XS = D_MODEL
OFF_BC = 2 * D_MODEL
OFF_DT = D_MODEL + SSM_CONV_DIM
OFF_GY = OFF_DT + SSM_HEADS
IN_DIM = OFF_GY + 4 * D_MODEL
PROJ_A_TAIL = 1280
PROJ_A_W = D_MODEL + PROJ_A_TAIL
PROJ_A_TN = PROJ_A_W // 2
A_SEC_Z = 0
A_BLK_BC = D_MODEL // SSM_BC
A_BLK_DT = (D_MODEL + SSM_BC) // LANES
B_SEC_GY, B_SEC_XL, B_SEC_GM, B_SEC_GR = range(4)
G_SEC_GY, G_SEC_GM, G_SEC_GR = range(3)

SSD_CHUNK = 128
MOE_ROWS = 512
VMEM_LIMIT = 56 * 1024 * 1024


def _cparams(sem):
    return pltpu.CompilerParams(dimension_semantics=sem, vmem_limit_bytes=VMEM_LIMIT)


def _rms(x, g):
    return x * lax.rsqrt(jnp.mean(x * x, axis=-1, keepdims=True) + RMS_EPS) * g


def _softplus(x):
    return jnp.maximum(x, 0.0) + jnp.log1p(jnp.exp(-jnp.abs(x)))


def _silu(x):
    return x * jax.nn.sigmoid(x)


def _gelu_tanh(x):
    c = 0.7978845608028654
    return x * (0.5 + 0.5 * jnp.tanh(x * (c + (c * 0.044715) * (x * x))))


def _split3(v):
    hi = v.astype(BF16)
    r1 = v - hi.astype(F32)
    mid = r1.astype(BF16)
    lo = (r1 - mid.astype(F32)).astype(BF16)
    return hi, mid, lo


def _dot(a, b):
    return jnp.dot(a, b, preferred_element_type=F32)


def _dot_exact_rhs(v, m_bf16):
    hi, mid, lo = _split3(v)
    return _dot(hi, m_bf16) + _dot(mid, m_bf16) + _dot(lo, m_bf16)


def _dot_exact_lhs(m_bf16, v):
    hi, mid, lo = _split3(v)
    return _dot(m_bf16, hi) + _dot(m_bf16, mid) + _dot(m_bf16, lo)


def _proj_a_kernel(x_ref, g_ref, w_ref, o_ref, xn_ref):
    xn = _rms(x_ref[...], g_ref[...]).astype(BF16)
    xn_ref[...] = xn
    for n0 in range(0, PROJ_A_W, PROJ_A_TN):
        o_ref[:, n0:n0 + PROJ_A_TN] = _dot(xn, w_ref[:, n0:n0 + PROJ_A_TN])


def _proj_a(x2d, g_mix, w_a, tm):
    t = x2d.shape[0]
    return pl.pallas_call(
        _proj_a_kernel,
        grid=(t // tm,),
        in_specs=[
            pl.BlockSpec((tm, D_MODEL), lambda i: (i, 0)),
            pl.BlockSpec((1, D_MODEL), lambda i: (0, 0)),
            pl.BlockSpec((D_MODEL, PROJ_A_W), lambda i: (0, 0), pipeline_mode=pl.Buffered(1)),
        ],
        out_specs=[
            pl.BlockSpec((tm, PROJ_A_W), lambda i: (i, 0)),
            pl.BlockSpec((tm, D_MODEL), lambda i: (i, 0)),
        ],
        out_shape=[
            jax.ShapeDtypeStruct((t, PROJ_A_W), F32),
            jax.ShapeDtypeStruct((t, D_MODEL), BF16),
        ],
        compiler_params=_cparams(("arbitrary",)),
        name="proj_a",
    )(x2d, g_mix, w_a)


def _proj_gates_kernel(xn_ref, w_ref, o_ref):
    j = pl.program_id(0)
    xn = xn_ref[...]

    def run(act):
        for n0 in range(0, D_MODEL, MXU_WIDTH):
            o_ref[:, n0:n0 + MXU_WIDTH] = act(_dot(xn, w_ref[:, n0:n0 + MXU_WIDTH]))

    @pl.when(j == G_SEC_GY)
    def _():
        run(_gelu_tanh)

    @pl.when(j != G_SEC_GY)
    def _():
        run(jax.nn.sigmoid)


def _proj_gates(xn2d, w_b, tm):
    t = xn2d.shape[0]
    skip_xl = lambda j: jnp.where(j >= B_SEC_XL, j + 1, j)
    return pl.pallas_call(
        _proj_gates_kernel,
        grid=(3, t // tm),
        in_specs=[
            pl.BlockSpec((tm, D_MODEL), lambda j, i: (i, 0)),
            pl.BlockSpec((D_MODEL, D_MODEL), lambda j, i: (0, skip_xl(j))),
        ],
        out_specs=pl.BlockSpec((tm, D_MODEL), lambda j, i: (i, j)),
        out_shape=jax.ShapeDtypeStruct((t, 3 * D_MODEL), F32),
        compiler_params=_cparams(("arbitrary", "arbitrary")),
        name="proj_gates",
    )(xn2d, w_b)


def _conv_tile(cbuf, lt, c0, c1, cw_ref, cb_ref):
    acc = cb_ref[:, c0:c1]
    for k in range(CONV_WIDTH):
        acc = acc + cbuf[SUBLANES - (CONV_WIDTH - 1) + k:SUBLANES - (CONV_WIDTH - 1) + k + lt, c0:c1] * cw_ref[k:k + 1, c0:c1]
    return acc


def _ssd_kernel(xn_ref, wxs_ref, bc_ref, dt_ref, z_ref, s0_ref, c0_ref, cw_ref, cb_ref, dtb_ref, alog_ref, dskip_ref,
                gssm_ref, rexp_ref, tril_ref,
                y_ref, snew_ref, cnew_ref,
                cbuf, xact, bcact, yscr, state, *, lt):
    q = SSD_CHUNK
    lp = max(lt, q)
    l = pl.program_id(1)

    @pl.when(l == 0)
    def _():
        cbuf[0:SUBLANES, :] = c0_ref[...]
        state[...] = s0_ref[...].T

    cbuf[SUBLANES:SUBLANES + lt, D_MODEL:SSM_CONV_DIM] = bc_ref[...]
    cw = 512
    xn = xn_ref[...]
    for c0 in range(0, D_MODEL, cw):
        cbuf[SUBLANES:SUBLANES + lt, c0:c0 + cw] = _dot(xn, wxs_ref[:, c0:c0 + cw])
        xact[:, c0:c0 + cw] = _silu(_conv_tile(cbuf, lt, c0, c0 + cw, cw_ref, cb_ref))
    for c0 in range(0, SSM_BC, cw):
        bcact[:, c0:c0 + cw] = _silu(_conv_tile(cbuf, lt, D_MODEL + c0, D_MODEL + c0 + cw, cw_ref, cb_ref))
    tail = cbuf[lt:lt + SUBLANES, :]
    cbuf[0:SUBLANES, :] = tail
    cnew_ref[...] = tail

    dt = _softplus(dt_ref[...] + dtb_ref[...])
    a_neg = -jnp.exp(alog_ref[...])
    da = dt * a_neg
    rexp = rexp_ref[...]
    dt_rep = _dot_exact_rhs(dt, rexp)
    tril = tril_ref[...]
    row_i = lax.broadcasted_iota(jnp.int32, (q, q), 0)
    col_i = lax.broadcasted_iota(jnp.int32, (q, q), 1)
    causal = row_i >= col_i
    lane_i = lax.broadcasted_iota(jnp.int32, (2 * q, LANES), 1)
    row2_i = lax.broadcasted_iota(jnp.int32, (2 * q, LANES), 0)
    pair_mask = (lane_i < SSM_HEAD_DIM) == (row2_i < q)

    def pad_rows(v):
        if lt == lp:
            return v
        return jnp.concatenate([v, jnp.zeros((lp - lt, v.shape[1]), v.dtype)], axis=0)

    for c in range(lp // q):
        r0 = c * q
        r1 = min(r0 + q, lt)
        da_c = pad_rows(da[r0:r1])
        acum = _dot_exact_lhs(tril, da_c)
        acum_t = acum.T
        acum_rep = _dot_exact_rhs(acum, rexp)
        last = acum_rep[q - 1:q, :]
        exp_a = jnp.exp(acum_rep)
        dec_out = jnp.exp(last - acum_rep)
        chunk_dec = jnp.exp(last)
        xa = pad_rows(xact[r0:r1, :])
        xdt = xa * pad_rows(dt_rep[r0:r1])
        xw = (xdt * dec_out).astype(BF16)
        xdt_b = xdt.astype(BF16)
        bca = pad_rows(bcact[r0:r1, :])
        for g in range(SSM_GROUPS):
            gw = SSM_HEADS_PER_GROUP * SSM_HEAD_DIM
            bg = bca[:, g * SSM_STATE:(g + 1) * SSM_STATE]
            cg = bca[:, SSM_GROUPS * SSM_STATE + g * SSM_STATE:SSM_GROUPS * SSM_STATE + (g + 1) * SSM_STATE]
            bg_b = bg.astype(BF16)
            cg_b = cg.astype(BF16)
            cb = lax.dot_general(cg_b, bg_b, (((1,), (1,)), ((), ())), preferred_element_type=F32)
            sg = state[:, g * gw:(g + 1) * gw]
            y_off = _dot(cg_b, sg.astype(BF16)) * exp_a[:, g * gw:(g + 1) * gw]
            for j in range(SSM_HEADS_PER_GROUP // 2):
                h0 = g * SSM_HEADS_PER_GROUP + 2 * j
                ms = []
                for h in (h0, h0 + 1):
                    seg = acum[:, h:h + 1] - acum_t[h:h + 1, :]
                    ms.append(cb * jnp.exp(jnp.where(causal, seg, -jnp.inf)))
                m = jnp.concatenate(ms, axis=1).astype(BF16)
                xp = xdt_b[:, h0 * SSM_HEAD_DIM:h0 * SSM_HEAD_DIM + LANES]
                rhs = jnp.where(pair_mask, jnp.concatenate([xp, xp], axis=0), jnp.zeros_like(xp[:1, :1]))
                yd = _dot(m, rhs)
                co = h0 * SSM_HEAD_DIM
                yv = yd + y_off[:, j * LANES:(j + 1) * LANES] + dskip_ref[:, co:co + LANES] * xa[:, co:co + LANES]
                yscr[r0:r1, co:co + LANES] = yv[0:r1 - r0]
            state[:, g * gw:(g + 1) * gw] = sg * chunk_dec[:, g * gw:(g + 1) * gw] + _dot(
                bg.T.astype(BF16), xw[:, g * gw:(g + 1) * gw])

    yz = yscr[...] * _silu(z_ref[...])
    y_ref[...] = _rms(yz, gssm_ref[...]).astype(BF16)

    @pl.when(l == pl.num_programs(1) - 1)
    def _():
        snew_ref[...] = state[...].T


def _ssd(xn3, w_xs, proj3, s0, c0p, cw8, cb, dtb, alog, dskip_rep, gssm, rexp, tril, lt):
    b, L, _ = proj3.shape
    nl = L // lt
    kern = functools.partial(_ssd_kernel, lt=lt)
    const = lambda bi, li: (0, 0)
    return pl.pallas_call(
        kern,
        grid=(b, nl),
        in_specs=[
            pl.BlockSpec((None, lt, D_MODEL), lambda bi, li: (bi, li, 0)),
            pl.BlockSpec((D_MODEL, D_MODEL), const, pipeline_mode=pl.Buffered(1)),
            pl.BlockSpec((None, lt, SSM_BC), lambda bi, li: (bi, li, A_BLK_BC)),
            pl.BlockSpec((None, lt, LANES), lambda bi, li: (bi, li, A_BLK_DT)),
            pl.BlockSpec((None, lt, D_MODEL), lambda bi, li: (bi, li, A_SEC_Z)),
            pl.BlockSpec((None, D_MODEL, SSM_STATE), lambda bi, li: (bi, 0, 0)),
            pl.BlockSpec((None, SUBLANES, SSM_CONV_DIM), lambda bi, li: (bi, 0, 0)),
            pl.BlockSpec((SUBLANES, SSM_CONV_DIM), const),
            pl.BlockSpec((1, SSM_CONV_DIM), const),
            pl.BlockSpec((1, LANES), const),
            pl.BlockSpec((1, LANES), const),
            pl.BlockSpec((1, D_MODEL), const),
            pl.BlockSpec((1, D_MODEL), const),
            pl.BlockSpec((LANES, D_MODEL), const),
            pl.BlockSpec((SSD_CHUNK, SSD_CHUNK), const),
        ],
        out_specs=[
            pl.BlockSpec((None, lt, D_MODEL), lambda bi, li: (bi, li, 0)),
            pl.BlockSpec((None, D_MODEL, SSM_STATE), lambda bi, li: (bi, 0, 0)),
            pl.BlockSpec((None, SUBLANES, SSM_CONV_DIM), lambda bi, li: (bi, 0, 0)),
        ],
        out_shape=[
            jax.ShapeDtypeStruct((b, L, D_MODEL), BF16),
            jax.ShapeDtypeStruct((b, D_MODEL, SSM_STATE), F32),
            jax.ShapeDtypeStruct((b, SUBLANES, SSM_CONV_DIM), F32),
        ],
        scratch_shapes=[
            pltpu.VMEM((lt + SUBLANES, SSM_CONV_DIM), F32),
            pltpu.VMEM((lt, D_MODEL), F32),
            pltpu.VMEM((lt, SSM_BC), F32),
            pltpu.VMEM((lt, D_MODEL), F32),
            pltpu.VMEM((SSM_STATE, D_MODEL), F32),
        ],
        compiler_params=_cparams(("arbitrary", "arbitrary")),
        name="ssd",
    )(xn3, w_xs, proj3, proj3, proj3, s0, c0p, cw8, cb, dtb, alog, dskip_rep, gssm, rexp, tril)


def _scan_rows(a, u, h_in):
    nv = a.shape[0] // SUBLANES
    a3 = a.reshape(nv, SUBLANES, LANES)
    u3 = u.reshape(nv, SUBLANES, LANES)
    sub = lax.broadcasted_iota(jnp.int32, a3.shape, 1)
    d = 1
    while d < SUBLANES:
        keep = sub >= d
        a_sh = jnp.where(keep, pltpu.roll(a3, d, 1), 1.0)
        u_sh = jnp.where(keep, pltpu.roll(u3, d, 1), 0.0)
        u3 = a3 * u_sh + u3
        a3 = a3 * a_sh
        d *= 2
    hs = []
    h = h_in
    for v in range(nv):
        hv = u3[v] + a3[v] * h
        hs.append(hv)
        h = hv[SUBLANES - 1:SUBLANES, :]
    return jnp.concatenate(hs, axis=0), h


def _lru_kernel(xn_ref, wxl_ref, ggy_ref, h0_ref, c0_ref, cw_ref, cb_ref, wax_ref, ba_ref, bx_ref, lam_ref,
                hg_ref, hnew_ref, cnew_ref,
                cbuf, hcar, *, lt):
    l = pl.program_id(1)

    @pl.when(l == 0)
    def _():
        cbuf[0:SUBLANES, :] = c0_ref[...]
        hcar[...] = h0_ref[...]

    cbuf[SUBLANES:SUBLANES + lt, :] = _dot(xn_ref[...], wxl_ref[...])
    for n in range(LRU_BLOCKS):
        c0 = n * LRU_BLOCK_WIDTH
        c1 = c0 + LRU_BLOCK_WIDTH
        xc = _conv_tile(cbuf, lt, c0, c1, cw_ref, cb_ref)
        pre = _dot(xc.astype(BF16), wax_ref[n])
        r = jax.nn.sigmoid(pre[:, 0:LRU_BLOCK_WIDTH] + ba_ref[:, c0:c1])
        i = jax.nn.sigmoid(pre[:, LRU_BLOCK_WIDTH:2 * LRU_BLOCK_WIDTH] + bx_ref[:, c0:c1])
        log_a = r * ((-LRU_C) * _softplus(-lam_ref[:, c0:c1]))
        a = jnp.exp(log_a)
        t = jnp.tanh(-log_a) * (a * a + 1.0)
        u = jnp.where(t > 0.0, t * lax.rsqrt(t), 0.0) * (i * xc)
        h, h_last = _scan_rows(a, u, hcar[:, c0:c1])
        hg_ref[:, c0:c1] = (h * ggy_ref[:, c0:c1]).astype(BF16)
        hcar[:, c0:c1] = h_last
        hnew_ref[:, c0:c1] = h_last
    tail = cbuf[lt:lt + SUBLANES, :]
    cbuf[0:SUBLANES, :] = tail
    cnew_ref[...] = tail


def _lru(xn3, w_b, gates3, h0, c0p, cw8, cb, wax, ba, bx, lam, lt):
    b, L, _ = xn3.shape
    nl = L // lt
    kern = functools.partial(_lru_kernel, lt=lt)
    const = lambda bi, li: (0, 0)
    return pl.pallas_call(
        kern,
        grid=(b, nl),
        in_specs=[
            pl.BlockSpec((None, lt, D_MODEL), lambda bi, li: (bi, li, 0)),
            pl.BlockSpec((D_MODEL, D_MODEL), lambda bi, li: (0, B_SEC_XL), pipeline_mode=pl.Buffered(1)),
            pl.BlockSpec((None, lt, D_MODEL), lambda bi, li: (bi, li, G_SEC_GY)),
            pl.BlockSpec((None, 1, D_MODEL), lambda bi, li: (bi, 0, 0)),
            pl.BlockSpec((None, SUBLANES, D_MODEL), lambda bi, li: (bi, 0, 0)),
            pl.BlockSpec((SUBLANES, D_MODEL), const),
            pl.BlockSpec((1, D_MODEL), const),
            pl.BlockSpec((LRU_BLOCKS, LRU_BLOCK_WIDTH, 2 * LRU_BLOCK_WIDTH), lambda bi, li: (0, 0, 0)),
            pl.BlockSpec((1, D_MODEL), const),
            pl.BlockSpec((1, D_MODEL), const),
            pl.BlockSpec((1, D_MODEL), const),
        ],
        out_specs=[
            pl.BlockSpec((None, lt, D_MODEL), lambda bi, li: (bi, li, 0)),
            pl.BlockSpec((None, 1, D_MODEL), lambda bi, li: (bi, 0, 0)),
            pl.BlockSpec((None, SUBLANES, D_MODEL), lambda bi, li: (bi, 0, 0)),
        ],
        out_shape=[
            jax.ShapeDtypeStruct((b, L, D_MODEL), BF16),
            jax.ShapeDtypeStruct((b, 1, D_MODEL), F32),
            jax.ShapeDtypeStruct((b, SUBLANES, D_MODEL), F32),
        ],
        scratch_shapes=[
            pltpu.VMEM((lt + SUBLANES, D_MODEL), F32),
            pltpu.VMEM((1, D_MODEL), F32),
        ],
        compiler_params=_cparams(("arbitrary", "arbitrary")),
        name="lru",
    )(xn3, w_b, gates3, h0, c0p, cw8, cb, wax, ba, bx, lam)


def _route(logits):
    neg = -1e30
    lane = lax.broadcasted_iota(jnp.int32, logits.shape, 1).astype(F32)
    is_g = lane < N_EXPERT_GROUPS
    gl = jnp.where(is_g, logits, neg)
    gmax = jnp.max(gl, axis=1, keepdims=True)
    gsum = jnp.sum(jnp.where(is_g, jnp.exp(gl - gmax), 0.0), axis=1, keepdims=True)
    g_w = 1.0 / gsum
    gidx = jnp.min(jnp.where(gl == gmax, lane, float(LANES)), axis=1, keepdims=True)
    lo = N_EXPERT_GROUPS + EXPERTS_PER_GROUP * gidx
    el = jnp.where((lane >= lo) & (lane < lo + EXPERTS_PER_GROUP), logits, neg)
    m1 = jnp.max(el, axis=1, keepdims=True)
    i1 = jnp.min(jnp.where(el == m1, lane, float(LANES)), axis=1, keepdims=True)
    el2 = jnp.where(lane == i1, neg, el)
    m2 = jnp.max(el2, axis=1, keepdims=True)
    i2 = jnp.min(jnp.where(el2 == m2, lane, float(LANES)), axis=1, keepdims=True)
    e2 = jnp.exp(m2 - m1)
    w1 = g_w / (1.0 + e2)
    w2 = g_w * e2 / (1.0 + e2)
    out = jnp.where(lane == 0, i1 - N_EXPERT_GROUPS, 0.0)
    out = jnp.where(lane == 1, i2 - N_EXPERT_GROUPS, out)
    out = jnp.where(lane == 2, w1, out)
    out = jnp.where(lane == 3, w2, out)
    return out


def _merge_kernel(x_ref, y_ref, hg_ref, gm_ref, gr_ref, wom_ref, wor_ref, wo_ref, gffn_ref, wr_ref, br_ref,
                  *rest, n_real):
    x1_ref, xn2_ref, route_ref = rest[-3:]
    i = pl.program_id(0)

    @pl.when(i < n_real)
    def _():
        o_m = _dot(y_ref[...], wom_ref[...])
        o_r = _dot(hg_ref[...], wor_ref[...])
        mixed = gm_ref[...] * o_m + gr_ref[...] * o_r
        x1 = x_ref[...] + _dot(mixed.astype(BF16), wo_ref[...])
        x1_ref[...] = x1
        xn2 = _rms(x1, gffn_ref[...])
        x_hi = xn2.astype(BF16)
        x_hi32 = x_hi.astype(F32)
        bits = lax.bitcast_convert_type(x_hi32, U32)
        xn2_ref[...] = (bits[:, :HALF_D] >> 16) | (bits[:, HALF_D:] & jnp.uint32(0xFFFF0000))
        x_lo = (xn2 - x_hi32).astype(BF16)
        logits = _dot(x_hi, wr_ref[0]) + _dot(x_lo, wr_ref[0]) + _dot(x_hi, wr_ref[1]) + br_ref[...]
        route_ref[...] = _route(logits)

    @pl.when(i >= n_real)
    def _():
        xn2_ref[...] = jnp.zeros_like(xn2_ref)
        route_ref[...] = jnp.zeros_like(route_ref)


def _merge(x2d, y2d, hg2d, proj2d, wom, wor, wo, gffn, wr, br, tm, t_all, row_off, shared):
    t = x2d.shape[0]
    n_real = t // tm
    off = row_off // tm
    n_extra = 0
    if shared is None and t_all > t:
        assert row_off == 0 and t_all - t <= tm
        n_extra = 1
    const = lambda i: (0, 0)
    cl = lambda i: jnp.minimum(i, n_real - 1)
    wspec = pl.BlockSpec((D_MODEL, D_MODEL), const, pipeline_mode=pl.Buffered(1))
    in_specs = [
        pl.BlockSpec((tm, D_MODEL), lambda i: (cl(i), 0)),
        pl.BlockSpec((tm, D_MODEL), lambda i: (cl(i), 0)),
        pl.BlockSpec((tm, D_MODEL), lambda i: (cl(i), 0)),
        pl.BlockSpec((tm, D_MODEL), lambda i: (cl(i), G_SEC_GM)),
        pl.BlockSpec((tm, D_MODEL), lambda i: (cl(i), G_SEC_GR)),
        wspec, wspec, wspec,
        pl.BlockSpec((1, D_MODEL), const),
        pl.BlockSpec((2, D_MODEL, LANES), lambda i: (0, 0, 0)),
        pl.BlockSpec((1, LANES), const),
    ]
    args = [x2d, y2d, hg2d, proj2d, proj2d, wom, wor, wo, gffn, wr, br]
    aliases = {}
    if shared is not None:
        in_specs += [pl.BlockSpec(memory_space=pl.ANY), pl.BlockSpec(memory_space=pl.ANY)]
        aliases = {len(args): 1, len(args) + 1: 2}
        args += list(shared)
    return pl.pallas_call(
        functools.partial(_merge_kernel, n_real=n_real),
        grid=(n_real + n_extra,),
        in_specs=in_specs,
        out_specs=[
            pl.BlockSpec((tm, D_MODEL), lambda i: (cl(i), 0)),
            pl.BlockSpec((tm, HALF_D), lambda i: (i + off, 0)),
            pl.BlockSpec((tm, LANES), lambda i: (i + off, 0)),
        ],
        out_shape=[
            jax.ShapeDtypeStruct((t, D_MODEL), F32),
            jax.ShapeDtypeStruct((t_all, HALF_D), U32),
            jax.ShapeDtypeStruct((t_all, LANES), F32),
        ],
        input_output_aliases=aliases,
        compiler_params=_cparams(("arbitrary",)),
        name="merge",
    )(*args)


def _moe_kernel(be_ref, nused_ref, tok_cur_ref, tok_nxt_ref, xn2_hbm, wg_ref, wu_ref, wd_ref, ys_ref,
                xbuf0, xbuf1, wgb, wub, wdb, sem):
    i = pl.program_id(0)
    n_used = nused_ref[0]
    xbufs = (xbuf0, xbuf1)

    def start_row(tok_ref, slot, r):
        pltpu.make_async_copy(xn2_hbm.at[pl.ds(tok_ref[0, r], 1), :], xbufs[slot].at[pl.ds(r, 1), :],
                              sem.at[slot]).start()

    def compute(slot, prefetch):
        pltpu.make_async_copy(xn2_hbm.at[pl.ds(0, MOE_ROWS), :], xbufs[slot], sem.at[slot]).wait()
        words = xbufs[slot][...]
        xa = lax.bitcast_convert_type(words << 16, F32).astype(BF16)
        xb = lax.bitcast_convert_type(words & jnp.uint32(0xFFFF0000), F32).astype(BF16)
        third = MOE_ROWS // 3

        def prefetch_rows(r0, r1):
            if prefetch:
                for r in range(r0, r1):
                    start_row(tok_nxt_ref, 1 - slot, r)

        prefetch_rows(0, third)
        hg = _dot(xa, wgb[0:HALF_D, :]) + _dot(xb, wgb[HALF_D:, :])
        prefetch_rows(third, 2 * third)
        hu = _dot(xa, wub[0:HALF_D, :]) + _dot(xb, wub[HALF_D:, :])
        prefetch_rows(2 * third, MOE_ROWS)
        ys_ref[...] = _dot((_silu(hg) * hu).astype(BF16), wdb[...])

    fresh = jnp.logical_or(i == 0, be_ref[i] != be_ref[jnp.maximum(i - 1, 0)])

    @pl.when(jnp.logical_and(fresh, i < n_used))
    def _():
        wgb[...] = wg_ref[...].astype(BF16)
        wub[...] = wu_ref[...].astype(BF16)
        wdb[...] = wd_ref[...].astype(BF16)

    @pl.when(jnp.logical_and(i == 0, n_used > 0))
    def _():
        def issue(r, c):
            start_row(tok_cur_ref, 0, r)
            return c
        lax.fori_loop(0, MOE_ROWS, issue, 0)

    for slot in (0, 1):
        mine = i % 2 == slot

        @pl.when(jnp.logical_and(mine, i + 1 < n_used))
        def _():
            compute(slot, True)

        @pl.when(jnp.logical_and(mine, jnp.logical_and(i < n_used, i + 1 >= n_used)))
        def _():
            compute(slot, False)

    @pl.when(i >= n_used)
    def _():
        ys_ref[...] = jnp.zeros_like(ys_ref)


def _moe(block_expert, n_used, row_token3, xn2_all, wg, wu, wd):
    n_blocks = row_token3.shape[0]
    grid_spec = pltpu.PrefetchScalarGridSpec(
        num_scalar_prefetch=2,
        grid=(n_blocks,),
        in_specs=[
            pl.BlockSpec((None, 1, MOE_ROWS), lambda i, be, nu: (i, 0, 0), memory_space=pltpu.SMEM),
            pl.BlockSpec((None, 1, MOE_ROWS), lambda i, be, nu: (jnp.minimum(i + 1, n_blocks - 1), 0, 0),
                         memory_space=pltpu.SMEM),
            pl.BlockSpec(memory_space=pl.ANY),
            pl.BlockSpec((None, D_MODEL, EXPERT_FF), lambda i, be, nu: (be[i], 0, 0)),
            pl.BlockSpec((None, D_MODEL, EXPERT_FF), lambda i, be, nu: (be[i], 0, 0)),
            pl.BlockSpec((None, EXPERT_FF, D_MODEL), lambda i, be, nu: (be[i], 0, 0)),
        ],
        out_specs=pl.BlockSpec((MOE_ROWS, D_MODEL), lambda i, be, nu: (i, 0)),
        scratch_shapes=[
            pltpu.VMEM((MOE_ROWS, HALF_D), U32),
            pltpu.VMEM((MOE_ROWS, HALF_D), U32),
            pltpu.VMEM((D_MODEL, EXPERT_FF), BF16),
            pltpu.VMEM((D_MODEL, EXPERT_FF), BF16),
            pltpu.VMEM((EXPERT_FF, D_MODEL), BF16),
            pltpu.SemaphoreType.DMA((2,)),
        ],
    )
    return pl.pallas_call(
        _moe_kernel,
        grid_spec=grid_spec,
        out_shape=jax.ShapeDtypeStruct((n_blocks * MOE_ROWS, D_MODEL), F32),
        compiler_params=_cparams(("arbitrary",)),
        name="moe",
    )(block_expert, n_used, row_token3, row_token3, xn2_all, wg, wu, wd)


def _final_kernel(d_cur_ref, d_nxt_ref, ys_hbm, x1_ref, route_ref, p_ref, gple_ref, wpg_ref, wple_ref, gfin_ref,
                  o_ref, gbuf0, gbuf1, sem, *, tm):
    i = pl.program_id(0)
    n = pl.num_programs(0)
    gbufs = (gbuf0, gbuf1)

    def start_row(d_ref, slot, r):
        pltpu.make_async_copy(ys_hbm.at[pl.ds(d_ref[0, r], 1), :], gbufs[slot].at[pl.ds(r, 1), :],
                              sem.at[slot]).start()

    def compute(slot, prefetch):
        gbuf = gbufs[slot]
        pltpu.make_async_copy(ys_hbm.at[pl.ds(0, TOP_K * tm), :], gbuf, sem.at[slot]).wait()
        route = route_ref[...]
        moe = route[:, 2:3] * gbuf[0:tm, :] + route[:, 3:4] * gbuf[tm:2 * tm, :]
        x2 = x1_ref[...] + moe
        xn3 = _rms(x2, gple_ref[...]).astype(BF16)
        if prefetch:
            for r in range(TOP_K * tm):
                start_row(d_nxt_ref, 1 - slot, r)
        gate = jax.nn.sigmoid(_dot(xn3, wpg_ref[...]))
        x3 = x2 + gate * _dot(p_ref[...].astype(BF16), wple_ref[...])
        o_ref[...] = _rms(x3, gfin_ref[...])

    @pl.when(i == 0)
    def _():
        def issue(r, c):
            start_row(d_cur_ref, 0, r)
            return c
        lax.fori_loop(0, TOP_K * tm, issue, 0)

    for slot in (0, 1):
        mine = i % 2 == slot

        @pl.when(jnp.logical_and(mine, i + 1 < n))
        def _():
            compute(slot, True)

        @pl.when(jnp.logical_and(mine, i + 1 >= n))
        def _():
            compute(slot, False)


def _final(dest3, ys, x1, route_all, p2d, gple, wpg, wple, gfin, tm, row_off):
    t = x1.shape[0]
    n = t // tm
    off = row_off // tm
    const = lambda i: (0, 0)
    kern = functools.partial(_final_kernel, tm=tm)
    return pl.pallas_call(
        kern,
        grid=(n,),
        in_specs=[
            pl.BlockSpec((None, 1, TOP_K * tm), lambda i: (i, 0, 0), memory_space=pltpu.SMEM),
            pl.BlockSpec((None, 1, TOP_K * tm), lambda i: (jnp.minimum(i + 1, n - 1), 0, 0), memory_space=pltpu.SMEM),
            pl.BlockSpec(memory_space=pl.ANY),
            pl.BlockSpec((tm, D_MODEL), lambda i: (i, 0)),
            pl.BlockSpec((tm, LANES), lambda i: (i + off, 0)),
            pl.BlockSpec((tm, PLE_DIM), lambda i: (i, 0)),
            pl.BlockSpec((1, D_MODEL), const),
            pl.BlockSpec((D_MODEL, D_MODEL), const, pipeline_mode=pl.Buffered(1)),
            pl.BlockSpec((PLE_DIM, D_MODEL), const),
            pl.BlockSpec((1, D_MODEL), const),
        ],
        out_specs=pl.BlockSpec((tm, D_MODEL), lambda i: (i, 0)),
        out_shape=jax.ShapeDtypeStruct((t, D_MODEL), F32),
        scratch_shapes=[
            pltpu.VMEM((TOP_K * tm, D_MODEL), F32),
            pltpu.VMEM((TOP_K * tm, D_MODEL), F32),
            pltpu.SemaphoreType.DMA((2,)),
        ],
        compiler_params=_cparams(("arbitrary",)),
        name="final",
    )(dest3, dest3, ys, x1, route_all, p2d, gple, wpg, wple, gfin)


def _pad_rows8(a, at_end=False):
    r = a.shape[-2]
    pad = [(0, 0)] * (a.ndim - 2) + ([(0, SUBLANES - r)] if at_end else [(SUBLANES - r, 0)]) + [(0, 0)]
    return jnp.pad(a, pad)


def _dispatch(route_all):
    t_all = route_all.shape[0]
    n_assign = t_all * TOP_K
    n_blocks = -(-n_assign // MOE_ROWS) + N_EXPERTS
    ids = jnp.arange(N_EXPERTS, dtype=jnp.int32)[None, :]
    onehots = [(route_all[:, k].astype(jnp.int32)[:, None] == ids).astype(jnp.int32) for k in range(TOP_K)]
    per_token = onehots[0] + onehots[1]
    csum = jnp.cumsum(per_token, axis=0)
    before = csum - per_token
    counts = csum[-1]
    nblk = (counts + MOE_ROWS - 1) // MOE_ROWS
    blk_end = jnp.cumsum(nblk)
    blk_start = blk_end - nblk
    slot0 = blk_start[None, :] * MOE_ROWS + before
    dest = jnp.stack([jnp.sum(oh * slot0, axis=1) for oh in onehots])
    token = jnp.tile(jnp.arange(t_all, dtype=jnp.int32), TOP_K)
    row_token = jnp.zeros((n_blocks * MOE_ROWS,), jnp.int32).at[dest.reshape(-1)].set(token)
    block_expert = jnp.minimum(jnp.sum(blk_end[None, :] <= jnp.arange(n_blocks, dtype=jnp.int32)[:, None], axis=1),
                               N_EXPERTS - 1).astype(jnp.int32)
    n_used = blk_end[-1:].astype(jnp.int32)
    return dest.astype(jnp.int32), row_token.reshape(n_blocks, 1, MOE_ROWS), block_expert, n_used


def _tile_dest(dest, tm):
    t = dest.shape[1]
    return dest.reshape(TOP_K, t // tm, tm).transpose(1, 0, 2).reshape(t // tm, 1, TOP_K * tm)


def kernel(x_prompt, x_sample, state_ssm, state_ssm_conv, state_lru, state_lru_conv, p_prompt, p_sample, g_mix, w_in, conv_m_w, conv_m_b, dt_bias, a_log, d_skip, g_ssm, w_out_m, conv_r_w, conv_r_b, w_lru_a, b_lru_a, w_lru_x, b_lru_x, lru_lambda, w_out_r, w_o, g_ffn, w_router_group, b_router_group, w_router_expert, b_router_expert, w_exp_gate, w_exp_up, w_exp_down, g_ple, w_ple_gate, w_ple, g_final):
    depth = w_in.shape[0]
    assert depth == 1, "one layer per step"
    bp, lp_, _ = x_prompt.shape
    bs, ls, _ = x_sample.shape
    tp, ts = bp * lp_, bs * ls
    t_all = tp + ts

    w = w_in[0]
    assert w.shape == (D_MODEL, IN_DIM)
    w_a = jnp.concatenate([w[:, 0:OFF_XS], w[:, OFF_BC:OFF_BC + PROJ_A_TAIL]], axis=1).astype(BF16)
    w_xs = w[:, OFF_XS:OFF_BC].astype(BF16)
    w_b = w[:, OFF_GY:IN_DIM].astype(BF16)
    row = lambda v: v.reshape(1, -1).astype(F32)
    pad_lanes = lambda v: jnp.pad(v.reshape(1, -1).astype(F32), ((0, 0), (0, LANES - v.shape[-1])))
    cw_m = _pad_rows8(conv_m_w[0].astype(F32), at_end=True)
    cw_r = _pad_rows8(conv_r_w[0].astype(F32), at_end=True)
    dskip_rep = jnp.repeat(d_skip[0].astype(F32), SSM_HEAD_DIM).reshape(1, D_MODEL)
    rexp = (jnp.arange(D_MODEL)[None, :] // SSM_HEAD_DIM == jnp.arange(LANES)[:, None]).astype(BF16)
    tril = jnp.tril(jnp.ones((SSD_CHUNK, SSD_CHUNK), F32)).astype(BF16)
    wax = jnp.concatenate([w_lru_a[0], w_lru_x[0]], axis=-1).astype(BF16)
    wom, wor, wo = w_out_m[0].astype(BF16), w_out_r[0].astype(BF16), w_o[0].astype(BF16)
    wr32 = jnp.pad(jnp.concatenate([w_router_group[0], w_router_expert[0]], axis=1).astype(F32),
                   ((0, 0), (0, LANES - N_EXPERT_GROUPS - N_EXPERTS)))
    wr_hi = wr32.astype(BF16)
    wr = jnp.stack([wr_hi, (wr32 - wr_hi.astype(F32)).astype(BF16)])
    br = pad_lanes(jnp.concatenate([b_router_group[0], b_router_expert[0]]))
    wg, wu, wd = w_exp_gate[0].astype(F32), w_exp_up[0].astype(F32), w_exp_down[0].astype(F32)
    wpg, wple = w_ple_gate[0].astype(BF16), w_ple[0].astype(BF16)

    def branches(x, ssm0, sconv0, lru0, lconv0, tm_proj, lt):
        b, L, _ = x.shape
        x2d = x.reshape(b * L, D_MODEL)
        proj_a, xn = _proj_a(x2d, row(g_mix[0]), w_a, tm_proj)
        gates = _proj_gates(xn, w_b, tm_proj)
        xn3 = xn.reshape(b, L, D_MODEL)
        y, ssm_new, sconv8 = _ssd(xn3, w_xs, proj_a.reshape(b, L, PROJ_A_W),
                                  ssm0.reshape(b, D_MODEL, SSM_STATE).astype(F32), _pad_rows8(sconv0.astype(F32)),
                                  cw_m, row(conv_m_b[0]), pad_lanes(dt_bias[0]), pad_lanes(a_log[0]), dskip_rep,
                                  row(g_ssm[0]), rexp, tril, lt)
        hg, lru_new, lconv8 = _lru(xn3, w_b, gates.reshape(b, L, 3 * D_MODEL),
                                   lru0.reshape(b, 1, D_MODEL).astype(F32), _pad_rows8(lconv0.astype(F32)),
                                   cw_r, row(conv_r_b[0]), wax, row(b_lru_a[0]), row(b_lru_x[0]), row(lru_lambda[0]), lt)
        states = (ssm_new.reshape(1, b, SSM_HEADS, SSM_HEAD_DIM, SSM_STATE),
                  sconv8[None, :, SUBLANES - (CONV_WIDTH - 1):, :],
                  lru_new.reshape(1, b, D_MODEL),
                  lconv8[None, :, SUBLANES - (CONV_WIDTH - 1):, :])
        return x2d, gates, y.reshape(b * L, D_MODEL), hg.reshape(b * L, D_MODEL), states

    zeros = lambda *s: jnp.zeros(s, F32)
    lt_p = min(256, lp_)
    tm_p = min(256, tp)
    tm_s = min(128, ts)
    xp2, proj_p, y_p, hg_p, st_p = branches(
        x_prompt, zeros(bp, SSM_HEADS, SSM_HEAD_DIM, SSM_STATE), zeros(bp, CONV_WIDTH - 1, SSM_CONV_DIM),
        zeros(bp, D_MODEL), zeros(bp, CONV_WIDTH - 1, D_MODEL), min(512, tp), lt_p)
    xs2, proj_s, y_s, hg_s, st_s = branches(
        x_sample, state_ssm[0], state_ssm_conv[0], state_lru[0], state_lru_conv[0], tm_s, ls)

    x1_p, xn2_all, route_all = _merge(xp2, y_p, hg_p, proj_p, wom, wor, wo, row(g_ffn[0]), wr, br, tm_p, t_all, 0, None)
    x1_s, xn2_all, route_all = _merge(xs2, y_s, hg_s, proj_s, wom, wor, wo, row(g_ffn[0]), wr, br, tm_s, t_all, tp,
                                      (xn2_all, route_all))

    dest, row_token3, block_expert, n_used = _dispatch(route_all)
    ys = _moe(block_expert, n_used, row_token3, xn2_all, wg, wu, wd)

    fin = lambda x1, d, p, tm, off: _final(_tile_dest(d, tm), ys, x1, route_all, p.reshape(-1, PLE_DIM).astype(F32),
                                           row(g_ple[0]), wpg, wple, row(g_final), tm, off)
    y_prompt = fin(x1_p, dest[:, :tp], p_prompt[0], tm_p, 0).reshape(x_prompt.shape)
    y_sample = fin(x1_s, dest[:, tp:], p_sample[0], tm_s, tp).reshape(x_sample.shape)
    return (y_prompt, y_sample) + st_p + st_s
```

```python
import functools

import jax
import jax.numpy as jnp
from jax import lax
from jax.experimental import pallas as pl
from jax.experimental.pallas import tpu as pltpu

F32 = jnp.float32
BF16 = jnp.bfloat16
U32 = jnp.uint32

D_MODEL = 2048
HALF_D = D_MODEL // 2
CONV_WIDTH = 4
RMS_EPS = 1e-6
SSM_HEADS = 32
SSM_HEAD_DIM = 64
SSM_GROUPS = 4
SSM_HEADS_PER_GROUP = 8
SSM_STATE = 128
SSM_BC = 2 * SSM_GROUPS * SSM_STATE
SSM_CONV_DIM = D_MODEL + SSM_BC
LRU_BLOCKS = 16
LRU_BLOCK_WIDTH = 128
LRU_C = 8.0
N_EXPERT_GROUPS = 4
EXPERTS_PER_GROUP = 8
N_EXPERTS = 32
TOP_K = 2
EXPERT_FF = 512
PLE_DIM = 256
LANES = 128
SUBLANES = 8
MXU_WIDTH = 256
DMA_QUEUES = 2

OFF_XS = D_MODEL
OFF_BC = 2 * D_MODEL
OFF_DT = D_MODEL + SSM_CONV_DIM
OFF_GY = OFF_DT + SSM_HEADS
IN_DIM = OFF_GY + 4 * D_MODEL
PROJ_A_TAIL = 1280
PROJ_A_W = D_MODEL + PROJ_A_TAIL
PROJ_A_TN = PROJ_A_W // 2
A_SEC_Z = 0
A_BLK_BC = D_MODEL // SSM_BC
A_BLK_DT = (D_MODEL + SSM_BC) // LANES
B_SEC_GY, B_SEC_XL, B_SEC_GM, B_SEC_GR = range(4)
G_SEC_GY, G_SEC_GM, G_SEC_GR = range(3)

SSD_CHUNK = 128
MOE_ROWS = 512
VMEM_LIMIT = 56 * 1024 * 1024


def _cparams(sem):
    return pltpu.CompilerParams(dimension_semantics=sem, vmem_limit_bytes=VMEM_LIMIT)


def _rms(x, g):
    return x * lax.rsqrt(jnp.mean(x * x, axis=-1, keepdims=True) + RMS_EPS) * g


def _softplus(x):
    return jnp.maximum(x, 0.0) + jnp.log1p(jnp.exp(-jnp.abs(x)))


def _silu(x):
    return x * jax.nn.sigmoid(x)


def _gelu_tanh(x):
    c = 0.7978845608028654
    return x * (0.5 + 0.5 * jnp.tanh(x * (c + (c * 0.044715) * (x * x))))


def _split3(v):
    hi = v.astype(BF16)
    r1 = v - hi.astype(F32)
    mid = r1.astype(BF16)
    lo = (r1 - mid.astype(F32)).astype(BF16)
    return hi, mid, lo


def _dot(a, b):
    return jnp.dot(a, b, preferred_element_type=F32)


def _dot_exact_rhs(v, m_bf16):
    hi, mid, lo = _split3(v)
    return _dot(hi, m_bf16) + _dot(mid, m_bf16) + _dot(lo, m_bf16)


def _dot_exact_lhs(m_bf16, v):
    hi, mid, lo = _split3(v)
    return _dot(m_bf16, hi) + _dot(m_bf16, mid) + _dot(m_bf16, lo)


def _proj_a_kernel(x_ref, g_ref, w_ref, o_ref, xn_ref):
    xn = _rms(x_ref[...], g_ref[...]).astype(BF16)
    xn_ref[...] = xn
    for n0 in range(0, PROJ_A_W, PROJ_A_TN):
        o_ref[:, n0:n0 + PROJ_A_TN] = _dot(xn, w_ref[:, n0:n0 + PROJ_A_TN])


def _proj_a(x2d, g_mix, w_a, tm):
    t = x2d.shape[0]
    return pl.pallas_call(
        _proj_a_kernel,
        grid=(t // tm,),
        in_specs=[
            pl.BlockSpec((tm, D_MODEL), lambda i: (i, 0)),
            pl.BlockSpec((1, D_MODEL), lambda i: (0, 0)),
            pl.BlockSpec((D_MODEL, PROJ_A_W), lambda i: (0, 0), pipeline_mode=pl.Buffered(1)),
        ],
        out_specs=[
            pl.BlockSpec((tm, PROJ_A_W), lambda i: (i, 0)),
            pl.BlockSpec((tm, D_MODEL), lambda i: (i, 0)),
        ],
        out_shape=[
            jax.ShapeDtypeStruct((t, PROJ_A_W), F32),
            jax.ShapeDtypeStruct((t, D_MODEL), BF16),
        ],
        compiler_params=_cparams(("arbitrary",)),
        name="proj_a",
    )(x2d, g_mix, w_a)


def _proj_gates_kernel(xn_ref, w_ref, o_ref):
    j = pl.program_id(0)
    xn = xn_ref[...]

    def run(act):
        for n0 in range(0, D_MODEL, MXU_WIDTH):
            o_ref[:, n0:n0 + MXU_WIDTH] = act(_dot(xn, w_ref[:, n0:n0 + MXU_WIDTH]))

    @pl.when(j == G_SEC_GY)
    def _():
        run(_gelu_tanh)

    @pl.when(j != G_SEC_GY)
    def _():
        run(jax.nn.sigmoid)


def _proj_gates(xn2d, w_b, tm):
    t = xn2d.shape[0]
    skip_xl = lambda j: jnp.where(j >= B_SEC_XL, j + 1, j)
    return pl.pallas_call(
        _proj_gates_kernel,
        grid=(3, t // tm),
        in_specs=[
            pl.BlockSpec((tm, D_MODEL), lambda j, i: (i, 0)),
            pl.BlockSpec((D_MODEL, D_MODEL), lambda j, i: (0, skip_xl(j))),
        ],
        out_specs=pl.BlockSpec((tm, D_MODEL), lambda j, i: (i, j)),
        out_shape=jax.ShapeDtypeStruct((t, 3 * D_MODEL), F32),
        compiler_params=_cparams(("arbitrary", "arbitrary")),
        name="proj_gates",
    )(xn2d, w_b)


def _conv_tile(cbuf, lt, c0, c1, cw_ref, cb_ref):
    acc = cb_ref[:, c0:c1]
    for k in range(CONV_WIDTH):
        acc = acc + cbuf[SUBLANES - (CONV_WIDTH - 1) + k:SUBLANES - (CONV_WIDTH - 1) + k + lt, c0:c1] * cw_ref[k:k + 1, c0:c1]
    return acc


def _ssd_kernel(xn_ref, xn_nxt_ref, wxs_ref, bc_ref, dt_ref, z_ref, s0_ref, c0_ref, cw_ref, cb_ref, dtb_ref, alog_ref,
                dskip_ref, gssm_ref, rexp_ref, tril_ref,
                y_ref, snew_ref, cnew_ref,
                cbuf, xs_nxt, xact, bcact, yscr, state, *, lt):
    q = SSD_CHUNK
    lp = max(lt, q)
    l = pl.program_id(1)

    @pl.when(l == 0)
    def _():
        cbuf[0:SUBLANES, :] = c0_ref[...]
        state[...] = s0_ref[...].T

    @pl.when(jnp.logical_and(pl.program_id(0) == 0, l == 0))
    def _():
        xs_nxt[...] = _dot(xn_ref[...], wxs_ref[...])

    cbuf[SUBLANES:SUBLANES + lt, 0:D_MODEL] = xs_nxt[...]
    xn_nxt = xn_nxt_ref[...]
    n_slices = (lp // q) * SSM_GROUPS
    slice_w = D_MODEL // n_slices
    cbuf[SUBLANES:SUBLANES + lt, D_MODEL:SSM_CONV_DIM] = bc_ref[...]
    cw = 512
    for c0 in range(0, D_MODEL, cw):
        xact[:, c0:c0 + cw] = _silu(_conv_tile(cbuf, lt, c0, c0 + cw, cw_ref, cb_ref))
    for c0 in range(0, SSM_BC, cw):
        bcact[:, c0:c0 + cw] = _silu(_conv_tile(cbuf, lt, D_MODEL + c0, D_MODEL + c0 + cw, cw_ref, cb_ref))
    tail = cbuf[lt:lt + SUBLANES, :]
    cbuf[0:SUBLANES, :] = tail
    cnew_ref[...] = tail

    dt = _softplus(dt_ref[...] + dtb_ref[...])
    a_neg = -jnp.exp(alog_ref[...])
    da = dt * a_neg
    rexp = rexp_ref[...]
    dt_rep = _dot_exact_rhs(dt, rexp)
    tril = tril_ref[...]
    row_i = lax.broadcasted_iota(jnp.int32, (q, q), 0)
    col_i = lax.broadcasted_iota(jnp.int32, (q, q), 1)
    causal = row_i >= col_i
    lane_i = lax.broadcasted_iota(jnp.int32, (2 * q, LANES), 1)
    row2_i = lax.broadcasted_iota(jnp.int32, (2 * q, LANES), 0)
    pair_mask = (lane_i < SSM_HEAD_DIM) == (row2_i < q)

    def pad_rows(v):
        if lt == lp:
            return v
        return jnp.concatenate([v, jnp.zeros((lp - lt, v.shape[1]), v.dtype)], axis=0)

    for c in range(lp // q):
        r0 = c * q
        r1 = min(r0 + q, lt)
        da_c = pad_rows(da[r0:r1])
        acum = _dot_exact_lhs(tril, da_c)
        acum_t = acum.T
        acum_rep = _dot_exact_rhs(acum, rexp)
        last = acum_rep[q - 1:q, :]
        exp_a = jnp.exp(acum_rep)
        dec_out = jnp.exp(last - acum_rep)
        chunk_dec = jnp.exp(last)
        xa = pad_rows(xact[r0:r1, :])
        xdt = xa * pad_rows(dt_rep[r0:r1])
        xw = (xdt * dec_out).astype(BF16)
        xdt_b = xdt.astype(BF16)
        bca = pad_rows(bcact[r0:r1, :])
        for g in range(SSM_GROUPS):
            s0 = (c * SSM_GROUPS + g) * slice_w
            xs_nxt[:, s0:s0 + slice_w] = _dot(xn_nxt, wxs_ref[:, s0:s0 + slice_w])
            gw = SSM_HEADS_PER_GROUP * SSM_HEAD_DIM
            bg = bca[:, g * SSM_STATE:(g + 1) * SSM_STATE]
            cg = bca[:, SSM_GROUPS * SSM_STATE + g * SSM_STATE:SSM_GROUPS * SSM_STATE + (g + 1) * SSM_STATE]
            bg_b = bg.astype(BF16)
            cg_b = cg.astype(BF16)
            cb = lax.dot_general(cg_b, bg_b, (((1,), (1,)), ((), ())), preferred_element_type=F32)
            sg = state[:, g * gw:(g + 1) * gw]
            y_off = _dot(cg_b, sg.astype(BF16)) * exp_a[:, g * gw:(g + 1) * gw]
            for j in range(SSM_HEADS_PER_GROUP // 2):
                h0 = g * SSM_HEADS_PER_GROUP + 2 * j
                ms = []
                for h in (h0, h0 + 1):
                    seg = acum[:, h:h + 1] - acum_t[h:h + 1, :]
                    ms.append(cb * jnp.exp(jnp.where(causal, seg, -jnp.inf)))
                m = jnp.concatenate(ms, axis=1).astype(BF16)
                xp = xdt_b[:, h0 * SSM_HEAD_DIM:h0 * SSM_HEAD_DIM + LANES]
                rhs = jnp.where(pair_mask, jnp.concatenate([xp, xp], axis=0), jnp.zeros_like(xp[:1, :1]))
                yd = _dot(m, rhs)
                co = h0 * SSM_HEAD_DIM
                yv = yd + y_off[:, j * LANES:(j + 1) * LANES] + dskip_ref[:, co:co + LANES] * xa[:, co:co + LANES]
                yscr[r0:r1, co:co + LANES] = yv[0:r1 - r0]
            state[:, g * gw:(g + 1) * gw] = sg * chunk_dec[:, g * gw:(g + 1) * gw] + _dot(
                bg.T.astype(BF16), xw[:, g * gw:(g + 1) * gw])

    yz = yscr[...] * _silu(z_ref[...])
    y_ref[...] = _rms(yz, gssm_ref[...]).astype(BF16)

    @pl.when(l == pl.num_programs(1) - 1)
    def _():
        snew_ref[...] = state[...].T


def _ssd(xn3, w_xs, proj3, s0, c0p, cw8, cb, dtb, alog, dskip_rep, gssm, rexp, tril, lt):
    b, L, _ = proj3.shape
    nl = L // lt
    kern = functools.partial(_ssd_kernel, lt=lt)
    const = lambda bi, li: (0, 0)
    return pl.pallas_call(
        kern,
        grid=(b, nl),
        in_specs=[
            pl.BlockSpec((None, lt, D_MODEL), lambda bi, li: (bi, li, 0)),
            pl.BlockSpec((None, lt, D_MODEL), _next_tile(b, nl)),
            pl.BlockSpec((D_MODEL, D_MODEL), const, pipeline_mode=pl.Buffered(1)),
            pl.BlockSpec((None, lt, SSM_BC), lambda bi, li: (bi, li, A_BLK_BC)),
            pl.BlockSpec((None, lt, LANES), lambda bi, li: (bi, li, A_BLK_DT)),
            pl.BlockSpec((None, lt, D_MODEL), lambda bi, li: (bi, li, A_SEC_Z)),
            pl.BlockSpec((None, D_MODEL, SSM_STATE), lambda bi, li: (bi, 0, 0)),
            pl.BlockSpec((None, SUBLANES, SSM_CONV_DIM), lambda bi, li: (bi, 0, 0)),
            pl.BlockSpec((SUBLANES, SSM_CONV_DIM), const),
            pl.BlockSpec((1, SSM_CONV_DIM), const),
            pl.BlockSpec((1, LANES), const),
            pl.BlockSpec((1, LANES), const),
            pl.BlockSpec((1, D_MODEL), const),
            pl.BlockSpec((1, D_MODEL), const),
            pl.BlockSpec((LANES, D_MODEL), const),
            pl.BlockSpec((SSD_CHUNK, SSD_CHUNK), const),
        ],
        out_specs=[
            pl.BlockSpec((None, lt, D_MODEL), lambda bi, li: (bi, li, 0)),
            pl.BlockSpec((None, D_MODEL, SSM_STATE), lambda bi, li: (bi, 0, 0)),
            pl.BlockSpec((None, SUBLANES, SSM_CONV_DIM), lambda bi, li: (bi, 0, 0)),
        ],
        out_shape=[
            jax.ShapeDtypeStruct((b, L, D_MODEL), BF16),
            jax.ShapeDtypeStruct((b, D_MODEL, SSM_STATE), F32),
            jax.ShapeDtypeStruct((b, SUBLANES, SSM_CONV_DIM), F32),
        ],
        scratch_shapes=[
            pltpu.VMEM((lt + SUBLANES, SSM_CONV_DIM), F32),
            pltpu.VMEM((lt, D_MODEL), F32),
            pltpu.VMEM((lt, D_MODEL), F32),
            pltpu.VMEM((lt, SSM_BC), F32),
            pltpu.VMEM((lt, D_MODEL), F32),
            pltpu.VMEM((SSM_STATE, D_MODEL), F32),
        ],
        compiler_params=_cparams(("arbitrary", "arbitrary")),
        name="ssd",
    )(xn3, xn3, w_xs, proj3, proj3, proj3, s0, c0p, cw8, cb, dtb, alog, dskip_rep, gssm, rexp, tril)


def _scan_rows(a, u, h_in):
    nv = a.shape[0] // SUBLANES
    a3 = a.reshape(nv, SUBLANES, LANES)
    u3 = u.reshape(nv, SUBLANES, LANES)
    sub = lax.broadcasted_iota(jnp.int32, a3.shape, 1)
    d = 1
    while d < SUBLANES:
        keep = sub >= d
        a_sh = jnp.where(keep, pltpu.roll(a3, d, 1), 1.0)
        u_sh = jnp.where(keep, pltpu.roll(u3, d, 1), 0.0)
        u3 = a3 * u_sh + u3
        a3 = a3 * a_sh
        d *= 2
    hs = []
    h = h_in
    for v in range(nv):
        hv = u3[v] + a3[v] * h
        hs.append(hv)
        h = hv[SUBLANES - 1:SUBLANES, :]
    return jnp.concatenate(hs, axis=0), h


def _lru_kernel(xn_ref, xn_nxt_ref, wxl_ref, ggy_ref, h0_ref, c0_ref, cw_ref, cb_ref, wax_ref, ba_ref, bx_ref,
                lam_ref, hg_ref, hnew_ref, cnew_ref,
                cbuf0, cbuf1, hcar, *, lt):
    l = pl.program_id(1)
    step = pl.program_id(0) * pl.num_programs(1) + l

    @pl.when(step == 0)
    def _():
        cbuf0[SUBLANES:SUBLANES + lt, :] = _dot(xn_ref[...], wxl_ref[...])

    for slot, (cur, nxt) in enumerate(((cbuf0, cbuf1), (cbuf1, cbuf0))):
        @pl.when(step % 2 == slot)
        def _():
            _lru_tile(cur, nxt, l, xn_nxt_ref, wxl_ref, ggy_ref, h0_ref, c0_ref, cw_ref, cb_ref, wax_ref, ba_ref,
                      bx_ref, lam_ref, hg_ref, hnew_ref, cnew_ref, hcar, lt)


def _lru_tile(cbuf, cbuf_nxt, l, xn_nxt_ref, wxl_ref, ggy_ref, h0_ref, c0_ref, cw_ref, cb_ref, wax_ref, ba_ref,
              bx_ref, lam_ref, hg_ref, hnew_ref, cnew_ref, hcar, lt):
    @pl.when(l == 0)
    def _():
        cbuf[0:SUBLANES, :] = c0_ref[...]
        hcar[...] = h0_ref[...]

    xn_nxt = xn_nxt_ref[...]
    for n in range(LRU_BLOCKS):
        c0 = n * LRU_BLOCK_WIDTH
        c1 = c0 + LRU_BLOCK_WIDTH
        if c0 % MXU_WIDTH == 0:
            cbuf_nxt[SUBLANES:SUBLANES + lt, c0:c0 + MXU_WIDTH] = _dot(xn_nxt, wxl_ref[:, c0:c0 + MXU_WIDTH])
        xc = _conv_tile(cbuf, lt, c0, c1, cw_ref, cb_ref)
        pre = _dot(xc.astype(BF16), wax_ref[n])
        r = jax.nn.sigmoid(pre[:, 0:LRU_BLOCK_WIDTH] + ba_ref[:, c0:c1])
        i = jax.nn.sigmoid(pre[:, LRU_BLOCK_WIDTH:2 * LRU_BLOCK_WIDTH] + bx_ref[:, c0:c1])
        log_a = r * ((-LRU_C) * _softplus(-lam_ref[:, c0:c1]))
        a = jnp.exp(log_a)
        t = jnp.tanh(-log_a) * (a * a + 1.0)
        u = jnp.where(t > 0.0, t * lax.rsqrt(t), 0.0) * (i * xc)
        h, h_last = _scan_rows(a, u, hcar[:, c0:c1])
        hg_ref[:, c0:c1] = (h * ggy_ref[:, c0:c1]).astype(BF16)
        hcar[:, c0:c1] = h_last
        hnew_ref[:, c0:c1] = h_last
    tail = cbuf[lt:lt + SUBLANES, :]
    cbuf_nxt[0:SUBLANES, :] = tail
    cnew_ref[...] = tail


def _next_tile(b, nl):
    def index(bi, li, *_):
        nxt = jnp.minimum(bi * nl + li + 1, b * nl - 1)
        return nxt // nl, nxt % nl, 0
    return index


def _lru(xn3, w_b, gates3, h0, c0p, cw8, cb, wax, ba, bx, lam, lt):
    b, L, _ = xn3.shape
    nl = L // lt
    kern = functools.partial(_lru_kernel, lt=lt)
    const = lambda bi, li: (0, 0)
    return pl.pallas_call(
        kern,
        grid=(b, nl),
        in_specs=[
            pl.BlockSpec((None, lt, D_MODEL), lambda bi, li: (bi, li, 0)),
            pl.BlockSpec((None, lt, D_MODEL), _next_tile(b, nl)),
            pl.BlockSpec((D_MODEL, D_MODEL), lambda bi, li: (0, B_SEC_XL), pipeline_mode=pl.Buffered(1)),
            pl.BlockSpec((None, lt, D_MODEL), lambda bi, li: (bi, li, G_SEC_GY)),
            pl.BlockSpec((None, 1, D_MODEL), lambda bi, li: (bi, 0, 0)),
            pl.BlockSpec((None, SUBLANES, D_MODEL), lambda bi, li: (bi, 0, 0)),
            pl.BlockSpec((SUBLANES, D_MODEL), const),
            pl.BlockSpec((1, D_MODEL), const),
            pl.BlockSpec((LRU_BLOCKS, LRU_BLOCK_WIDTH, 2 * LRU_BLOCK_WIDTH), lambda bi, li: (0, 0, 0)),
            pl.BlockSpec((1, D_MODEL), const),
            pl.BlockSpec((1, D_MODEL), const),
            pl.BlockSpec((1, D_MODEL), const),
        ],
        out_specs=[
            pl.BlockSpec((None, lt, D_MODEL), lambda bi, li: (bi, li, 0)),
            pl.BlockSpec((None, 1, D_MODEL), lambda bi, li: (bi, 0, 0)),
            pl.BlockSpec((None, SUBLANES, D_MODEL), lambda bi, li: (bi, 0, 0)),
        ],
        out_shape=[
            jax.ShapeDtypeStruct((b, L, D_MODEL), BF16),
            jax.ShapeDtypeStruct((b, 1, D_MODEL), F32),
            jax.ShapeDtypeStruct((b, SUBLANES, D_MODEL), F32),
        ],
        scratch_shapes=[
            pltpu.VMEM((lt + SUBLANES, D_MODEL), F32),
            pltpu.VMEM((lt + SUBLANES, D_MODEL), F32),
            pltpu.VMEM((1, D_MODEL), F32),
        ],
        compiler_params=_cparams(("arbitrary", "arbitrary")),
        name="lru",
    )(xn3, xn3, w_b, gates3, h0, c0p, cw8, cb, wax, ba, bx, lam)


def _route(logits):
    neg = -1e30
    lane = lax.broadcasted_iota(jnp.int32, logits.shape, 1).astype(F32)
    is_g = lane < N_EXPERT_GROUPS
    gl = jnp.where(is_g, logits, neg)
    gmax = jnp.max(gl, axis=1, keepdims=True)
    gsum = jnp.sum(jnp.where(is_g, jnp.exp(gl - gmax), 0.0), axis=1, keepdims=True)
    g_w = 1.0 / gsum
    gidx = jnp.min(jnp.where(gl == gmax, lane, float(LANES)), axis=1, keepdims=True)
    lo = N_EXPERT_GROUPS + EXPERTS_PER_GROUP * gidx
    el = jnp.where((lane >= lo) & (lane < lo + EXPERTS_PER_GROUP), logits, neg)
    m1 = jnp.max(el, axis=1, keepdims=True)
    i1 = jnp.min(jnp.where(el == m1, lane, float(LANES)), axis=1, keepdims=True)
    el2 = jnp.where(lane == i1, neg, el)
    m2 = jnp.max(el2, axis=1, keepdims=True)
    i2 = jnp.min(jnp.where(el2 == m2, lane, float(LANES)), axis=1, keepdims=True)
    e2 = jnp.exp(m2 - m1)
    w1 = g_w / (1.0 + e2)
    w2 = g_w * e2 / (1.0 + e2)
    out = jnp.where(lane == 0, i1 - N_EXPERT_GROUPS, 0.0)
    out = jnp.where(lane == 1, i2 - N_EXPERT_GROUPS, out)
    out = jnp.where(lane == 2, w1, out)
    out = jnp.where(lane == 3, w2, out)
    return out


def _merge_kernel(x_ref, y_ref, hg_ref, gm_ref, gr_ref, wom_ref, wor_ref, wo_ref, gffn_ref, wr_ref, br_ref,
                  *rest, n_real):
    x1_ref, xn2_ref, route_ref = rest[-3:]
    i = pl.program_id(0)

    @pl.when(i < n_real)
    def _():
        o_m = _dot(y_ref[...], wom_ref[...])
        o_r = _dot(hg_ref[...], wor_ref[...])
        mixed = gm_ref[...] * o_m + gr_ref[...] * o_r
        x1 = x_ref[...] + _dot(mixed.astype(BF16), wo_ref[...])
        x1_ref[...] = x1
        xn2 = _rms(x1, gffn_ref[...])
        x_hi = xn2.astype(BF16)
        x_hi32 = x_hi.astype(F32)
        bits = lax.bitcast_convert_type(x_hi32, U32)
        xn2_ref[...] = (bits[:, :HALF_D] >> 16) | (bits[:, HALF_D:] & jnp.uint32(0xFFFF0000))
        x_lo = (xn2 - x_hi32).astype(BF16)
        logits = _dot(x_hi, wr_ref[0]) + _dot(x_lo, wr_ref[0]) + _dot(x_hi, wr_ref[1]) + br_ref[...]
        route_ref[...] = _route(logits)

    @pl.when(i >= n_real)
    def _():
        xn2_ref[...] = jnp.zeros_like(xn2_ref)
        route_ref[...] = jnp.zeros_like(route_ref)


def _merge(x2d, y2d, hg2d, proj2d, wom, wor, wo, gffn, wr, br, tm, t_all, row_off, shared):
    t = x2d.shape[0]
    n_real = t // tm
    off = row_off // tm
    n_extra = 0
    if shared is None and t_all > t:
        assert row_off == 0 and t_all - t <= tm
        n_extra = 1
    const = lambda i: (0, 0)
    cl = lambda i: jnp.minimum(i, n_real - 1)
    wspec = pl.BlockSpec((D_MODEL, D_MODEL), const, pipeline_mode=pl.Buffered(1))
    in_specs = [
        pl.BlockSpec((tm, D_MODEL), lambda i: (cl(i), 0)),
        pl.BlockSpec((tm, D_MODEL), lambda i: (cl(i), 0)),
        pl.BlockSpec((tm, D_MODEL), lambda i: (cl(i), 0)),
        pl.BlockSpec((tm, D_MODEL), lambda i: (cl(i), G_SEC_GM)),
        pl.BlockSpec((tm, D_MODEL), lambda i: (cl(i), G_SEC_GR)),
        wspec, wspec, wspec,
        pl.BlockSpec((1, D_MODEL), const),
        pl.BlockSpec((2, D_MODEL, LANES), lambda i: (0, 0, 0)),
        pl.BlockSpec((1, LANES), const),
    ]
    args = [x2d, y2d, hg2d, proj2d, proj2d, wom, wor, wo, gffn, wr, br]
    aliases = {}
    if shared is not None:
        in_specs += [pl.BlockSpec(memory_space=pl.ANY), pl.BlockSpec(memory_space=pl.ANY)]
        aliases = {len(args): 1, len(args) + 1: 2}
        args += list(shared)
    return pl.pallas_call(
        functools.partial(_merge_kernel, n_real=n_real),
        grid=(n_real + n_extra,),
        in_specs=in_specs,
        out_specs=[
            pl.BlockSpec((tm, D_MODEL), lambda i: (cl(i), 0)),
            pl.BlockSpec((tm, HALF_D), lambda i: (i + off, 0)),
            pl.BlockSpec((tm, LANES), lambda i: (i + off, 0)),
        ],
        out_shape=[
            jax.ShapeDtypeStruct((t, D_MODEL), F32),
            jax.ShapeDtypeStruct((t_all, HALF_D), U32),
            jax.ShapeDtypeStruct((t_all, LANES), F32),
        ],
        input_output_aliases=aliases,
        compiler_params=_cparams(("arbitrary",)),
        name="merge",
    )(*args)


def _moe_kernel(be_ref, nused_ref, tok_cur_ref, tok_nxt_ref, xn2_hbm, wg_ref, wu_ref, wd_ref, ys_ref,
                xbuf0, xbuf1, wgb, wub, wdb, sem):
    i = pl.program_id(0)
    n_used = nused_ref[0]
    xbufs = (xbuf0, xbuf1)

    def start_row(tok_ref, slot, r, queue=0):
        pltpu.async_copy(xn2_hbm.at[pl.ds(tok_ref[0, r], 1), :], xbufs[slot].at[pl.ds(r, 1), :],
                         sem.at[slot], priority=queue)

    def compute(slot, prefetch):
        pltpu.make_async_copy(xn2_hbm.at[pl.ds(0, MOE_ROWS), :], xbufs[slot], sem.at[slot]).wait()
        words = xbufs[slot][...]
        xa = lax.bitcast_convert_type(words << 16, F32).astype(BF16)
        xb = lax.bitcast_convert_type(words & jnp.uint32(0xFFFF0000), F32).astype(BF16)
        third = MOE_ROWS // 3

        def prefetch_rows(r0, r1):
            if prefetch:
                for r in range(r0, r1):
                    start_row(tok_nxt_ref, 1 - slot, r, queue=r % DMA_QUEUES)

        prefetch_rows(0, third)
        hg = _dot(xa, wgb[0:HALF_D, :]) + _dot(xb, wgb[HALF_D:, :])
        prefetch_rows(third, 2 * third)
        hu = _dot(xa, wub[0:HALF_D, :]) + _dot(xb, wub[HALF_D:, :])
        prefetch_rows(2 * third, MOE_ROWS)
        ys_ref[...] = _dot((_silu(hg) * hu).astype(BF16), wdb[...])

    fresh = jnp.logical_or(i == 0, be_ref[i] != be_ref[jnp.maximum(i - 1, 0)])

    @pl.when(jnp.logical_and(fresh, i < n_used))
    def _():
        wgb[...] = wg_ref[...].astype(BF16)
        wub[...] = wu_ref[...].astype(BF16)
        wdb[...] = wd_ref[...].astype(BF16)

    @pl.when(jnp.logical_and(i == 0, n_used > 0))
    def _():
        def issue(r, c):
            start_row(tok_cur_ref, 0, r)
            return c
        lax.fori_loop(0, MOE_ROWS, issue, 0)

    for slot in (0, 1):
        mine = i % 2 == slot

        @pl.when(jnp.logical_and(mine, i + 1 < n_used))
        def _():
            compute(slot, True)

        @pl.when(jnp.logical_and(mine, jnp.logical_and(i < n_used, i + 1 >= n_used)))
        def _():
            compute(slot, False)

    @pl.when(i >= n_used)
    def _():
        ys_ref[...] = jnp.zeros_like(ys_ref)


def _moe(block_expert, n_used, row_token3, xn2_all, wg, wu, wd):
    n_blocks = row_token3.shape[0]
    grid_spec = pltpu.PrefetchScalarGridSpec(
        num_scalar_prefetch=2,
        grid=(n_blocks,),
        in_specs=[
            pl.BlockSpec((None, 1, MOE_ROWS), lambda i, be, nu: (i, 0, 0), memory_space=pltpu.SMEM),
            pl.BlockSpec((None, 1, MOE_ROWS), lambda i, be, nu: (jnp.minimum(i + 1, n_blocks - 1), 0, 0),
                         memory_space=pltpu.SMEM),
            pl.BlockSpec(memory_space=pl.ANY),
            pl.BlockSpec((None, D_MODEL, EXPERT_FF), lambda i, be, nu: (be[i], 0, 0)),
            pl.BlockSpec((None, D_MODEL, EXPERT_FF), lambda i, be, nu: (be[i], 0, 0)),
            pl.BlockSpec((None, EXPERT_FF, D_MODEL), lambda i, be, nu: (be[i], 0, 0)),
        ],
        out_specs=pl.BlockSpec((MOE_ROWS, D_MODEL), lambda i, be, nu: (i, 0)),
        scratch_shapes=[
            pltpu.VMEM((MOE_ROWS, HALF_D), U32),
            pltpu.VMEM((MOE_ROWS, HALF_D), U32),
            pltpu.VMEM((D_MODEL, EXPERT_FF), BF16),
            pltpu.VMEM((D_MODEL, EXPERT_FF), BF16),
            pltpu.VMEM((EXPERT_FF, D_MODEL), BF16),
            pltpu.SemaphoreType.DMA((2,)),
        ],
    )
    return pl.pallas_call(
        _moe_kernel,
        grid_spec=grid_spec,
        out_shape=jax.ShapeDtypeStruct((n_blocks * MOE_ROWS, D_MODEL), F32),
        compiler_params=_cparams(("arbitrary",)),
        name="moe",
    )(block_expert, n_used, row_token3, row_token3, xn2_all, wg, wu, wd)


def _final_kernel(d_cur_ref, d_nxt_ref, ys_hbm, x1_ref, route_ref, p_ref, gple_ref, wpg_ref, wple_ref, gfin_ref,
                  o_ref, gbuf0, gbuf1, sem, *, tm):
    i = pl.program_id(0)
    n = pl.num_programs(0)
    gbufs = (gbuf0, gbuf1)

    def start_row(d_ref, slot, r, queue=0):
        pltpu.async_copy(ys_hbm.at[pl.ds(d_ref[0, r], 1), :], gbufs[slot].at[pl.ds(r, 1), :],
                         sem.at[slot], priority=queue)

    def compute(slot, prefetch):
        gbuf = gbufs[slot]
        pltpu.make_async_copy(ys_hbm.at[pl.ds(0, TOP_K * tm), :], gbuf, sem.at[slot]).wait()
        route = route_ref[...]
        moe = route[:, 2:3] * gbuf[0:tm, :] + route[:, 3:4] * gbuf[tm:2 * tm, :]
        x2 = x1_ref[...] + moe
        xn3 = _rms(x2, gple_ref[...]).astype(BF16)
        if prefetch:
            for r in range(TOP_K * tm):
                start_row(d_nxt_ref, 1 - slot, r, queue=r % DMA_QUEUES)
        gate = jax.nn.sigmoid(_dot(xn3, wpg_ref[...]))
        x3 = x2 + gate * _dot(p_ref[...].astype(BF16), wple_ref[...])
        o_ref[...] = _rms(x3, gfin_ref[...])

    @pl.when(i == 0)
    def _():
        def issue(r, c):
            start_row(d_cur_ref, 0, r)
            return c
        lax.fori_loop(0, TOP_K * tm, issue, 0)

    for slot in (0, 1):
        mine = i % 2 == slot

        @pl.when(jnp.logical_and(mine, i + 1 < n))
        def _():
            compute(slot, True)

        @pl.when(jnp.logical_and(mine, i + 1 >= n))
        def _():
            compute(slot, False)


def _final(dest3, ys, x1, route_all, p2d, gple, wpg, wple, gfin, tm, row_off):
    t = x1.shape[0]
    n = t // tm
    off = row_off // tm
    const = lambda i: (0, 0)
    kern = functools.partial(_final_kernel, tm=tm)
    return pl.pallas_call(
        kern,
        grid=(n,),
        in_specs=[
            pl.BlockSpec((None, 1, TOP_K * tm), lambda i: (i, 0, 0), memory_space=pltpu.SMEM),
            pl.BlockSpec((None, 1, TOP_K * tm), lambda i: (jnp.minimum(i + 1, n - 1), 0, 0), memory_space=pltpu.SMEM),
            pl.BlockSpec(memory_space=pl.ANY),
            pl.BlockSpec((tm, D_MODEL), lambda i: (i, 0)),
            pl.BlockSpec((tm, LANES), lambda i: (i + off, 0)),
            pl.BlockSpec((tm, PLE_DIM), lambda i: (i, 0)),
            pl.BlockSpec((1, D_MODEL), const),
            pl.BlockSpec((D_MODEL, D_MODEL), const, pipeline_mode=pl.Buffered(1)),
            pl.BlockSpec((PLE_DIM, D_MODEL), const),
            pl.BlockSpec((1, D_MODEL), const),
        ],
        out_specs=pl.BlockSpec((tm, D_MODEL), lambda i: (i, 0)),
        out_shape=jax.ShapeDtypeStruct((t, D_MODEL), F32),
        scratch_shapes=[
            pltpu.VMEM((TOP_K * tm, D_MODEL), F32),
            pltpu.VMEM((TOP_K * tm, D_MODEL), F32),
            pltpu.SemaphoreType.DMA((2,)),
        ],
        compiler_params=_cparams(("arbitrary",)),
        name="final",
    )(dest3, dest3, ys, x1, route_all, p2d, gple, wpg, wple, gfin)


def _pad_rows8(a, at_end=False):
    r = a.shape[-2]
    pad = [(0, 0)] * (a.ndim - 2) + ([(0, SUBLANES - r)] if at_end else [(SUBLANES - r, 0)]) + [(0, 0)]
    return jnp.pad(a, pad)


def _dispatch(route_all):
    t_all = route_all.shape[0]
    n_assign = t_all * TOP_K
    n_blocks = -(-n_assign // MOE_ROWS) + N_EXPERTS
    ids = jnp.arange(N_EXPERTS, dtype=jnp.int32)[None, :]
    onehots = [(route_all[:, k].astype(jnp.int32)[:, None] == ids).astype(jnp.int32) for k in range(TOP_K)]
    per_token = onehots[0] + onehots[1]
    csum = jnp.cumsum(per_token, axis=0)
    before = csum - per_token
    counts = csum[-1]
    nblk = (counts + MOE_ROWS - 1) // MOE_ROWS
    blk_end = jnp.cumsum(nblk)
    blk_start = blk_end - nblk
    slot0 = blk_start[None, :] * MOE_ROWS + before
    dest = jnp.stack([jnp.sum(oh * slot0, axis=1) for oh in onehots])
    token = jnp.tile(jnp.arange(t_all, dtype=jnp.int32), TOP_K)
    row_token = jnp.zeros((n_blocks * MOE_ROWS,), jnp.int32).at[dest.reshape(-1)].set(token)
    block_expert = jnp.minimum(jnp.sum(blk_end[None, :] <= jnp.arange(n_blocks, dtype=jnp.int32)[:, None], axis=1),
                               N_EXPERTS - 1).astype(jnp.int32)
    n_used = blk_end[-1:].astype(jnp.int32)
    return dest.astype(jnp.int32), row_token.reshape(n_blocks, 1, MOE_ROWS), block_expert, n_used


def _tile_dest(dest, tm):
    t = dest.shape[1]
    return dest.reshape(TOP_K, t // tm, tm).transpose(1, 0, 2).reshape(t // tm, 1, TOP_K * tm)


def kernel(x_prompt, x_sample, state_ssm, state_ssm_conv, state_lru, state_lru_conv, p_prompt, p_sample, g_mix, w_in, conv_m_w, conv_m_b, dt_bias, a_log, d_skip, g_ssm, w_out_m, conv_r_w, conv_r_b, w_lru_a, b_lru_a, w_lru_x, b_lru_x, lru_lambda, w_out_r, w_o, g_ffn, w_router_group, b_router_group, w_router_expert, b_router_expert, w_exp_gate, w_exp_up, w_exp_down, g_ple, w_ple_gate, w_ple, g_final):
    depth = w_in.shape[0]
    assert depth == 1, "one layer per step"
    bp, lp_, _ = x_prompt.shape
    bs, ls, _ = x_sample.shape
    tp, ts = bp * lp_, bs * ls
    t_all = tp + ts

    w = w_in[0]
    assert w.shape == (D_MODEL, IN_DIM)
    w_a = jnp.concatenate([w[:, 0:OFF_XS], w[:, OFF_BC:OFF_BC + PROJ_A_TAIL]], axis=1).astype(BF16)
    w_xs = w[:, OFF_XS:OFF_BC].astype(BF16)
    w_b = w[:, OFF_GY:IN_DIM].astype(BF16)
    row = lambda v: v.reshape(1, -1).astype(F32)
    pad_lanes = lambda v: jnp.pad(v.reshape(1, -1).astype(F32), ((0, 0), (0, LANES - v.shape[-1])))
    cw_m = _pad_rows8(conv_m_w[0].astype(F32), at_end=True)
    cw_r = _pad_rows8(conv_r_w[0].astype(F32), at_end=True)
    dskip_rep = jnp.repeat(d_skip[0].astype(F32), SSM_HEAD_DIM).reshape(1, D_MODEL)
    rexp = (jnp.arange(D_MODEL)[None, :] // SSM_HEAD_DIM == jnp.arange(LANES)[:, None]).astype(BF16)
    tril = jnp.tril(jnp.ones((SSD_CHUNK, SSD_CHUNK), F32)).astype(BF16)
    wax = jnp.concatenate([w_lru_a[0], w_lru_x[0]], axis=-1).astype(BF16)
    wom, wor, wo = w_out_m[0].astype(BF16), w_out_r[0].astype(BF16), w_o[0].astype(BF16)
    wr32 = jnp.pad(jnp.concatenate([w_router_group[0], w_router_expert[0]], axis=1).astype(F32),
                   ((0, 0), (0, LANES - N_EXPERT_GROUPS - N_EXPERTS)))
    wr_hi = wr32.astype(BF16)
    wr = jnp.stack([wr_hi, (wr32 - wr_hi.astype(F32)).astype(BF16)])
    br = pad_lanes(jnp.concatenate([b_router_group[0], b_router_expert[0]]))
    wg, wu, wd = w_exp_gate[0].astype(F32), w_exp_up[0].astype(F32), w_exp_down[0].astype(F32)
    wpg, wple = w_ple_gate[0].astype(BF16), w_ple[0].astype(BF16)

    def branches(x, ssm0, sconv0, lru0, lconv0, tm_proj, lt):
        b, L, _ = x.shape
        x2d = x.reshape(b * L, D_MODEL)
        proj_a, xn = _proj_a(x2d, row(g_mix[0]), w_a, tm_proj)
        gates = _proj_gates(xn, w_b, tm_proj)
        xn3 = xn.reshape(b, L, D_MODEL)
        y, ssm_new, sconv8 = _ssd(xn3, w_xs, proj_a.reshape(b, L, PROJ_A_W),
                                  ssm0.reshape(b, D_MODEL, SSM_STATE).astype(F32), _pad_rows8(sconv0.astype(F32)),
                                  cw_m, row(conv_m_b[0]), pad_lanes(dt_bias[0]), pad_lanes(a_log[0]), dskip_rep,
                                  row(g_ssm[0]), rexp, tril, lt)
        hg, lru_new, lconv8 = _lru(xn3, w_b, gates.reshape(b, L, 3 * D_MODEL),
                                   lru0.reshape(b, 1, D_MODEL).astype(F32), _pad_rows8(lconv0.astype(F32)),
                                   cw_r, row(conv_r_b[0]), wax, row(b_lru_a[0]), row(b_lru_x[0]), row(lru_lambda[0]), lt)
        states = (ssm_new.reshape(1, b, SSM_HEADS, SSM_HEAD_DIM, SSM_STATE),
                  sconv8[None, :, SUBLANES - (CONV_WIDTH - 1):, :],
                  lru_new.reshape(1, b, D_MODEL),
                  lconv8[None, :, SUBLANES - (CONV_WIDTH - 1):, :])
        return x2d, gates, y.reshape(b * L, D_MODEL), hg.reshape(b * L, D_MODEL), states

    zeros = lambda *s: jnp.zeros(s, F32)
    lt_p = min(256, lp_)
    tm_p = min(256, tp)
    tm_s = min(128, ts)
    xp2, proj_p, y_p, hg_p, st_p = branches(
        x_prompt, zeros(bp, SSM_HEADS, SSM_HEAD_DIM, SSM_STATE), zeros(bp, CONV_WIDTH - 1, SSM_CONV_DIM),
        zeros(bp, D_MODEL), zeros(bp, CONV_WIDTH - 1, D_MODEL), min(512, tp), lt_p)
    xs2, proj_s, y_s, hg_s, st_s = branches(
        x_sample, state_ssm[0], state_ssm_conv[0], state_lru[0], state_lru_conv[0], tm_s, ls)

    x1_p, xn2_all, route_all = _merge(xp2, y_p, hg_p, proj_p, wom, wor, wo, row(g_ffn[0]), wr, br, tm_p, t_all, 0, None)
    x1_s, xn2_all, route_all = _merge(xs2, y_s, hg_s, proj_s, wom, wor, wo, row(g_ffn[0]), wr, br, tm_s, t_all, tp,
                                      (xn2_all, route_all))

    dest, row_token3, block_expert, n_used = _dispatch(route_all)
    ys = _moe(block_expert, n_used, row_token3, xn2_all, wg, wu, wd)

    fin = lambda x1, d, p, tm, off: _final(_tile_dest(d, tm), ys, x1, route_all, p.reshape(-1, PLE_DIM).astype(F32),
                                           row(g_ple[0]), wpg, wple, row(g_final), tm, off)
    y_prompt = fin(x1_p, dest[:, :tp], p_prompt[0], tm_p, 0).reshape(x_prompt.shape)
    y_sample = fin(x1_s, dest[:, tp:], p_sample[0], tm_s, tp).reshape(x_sample.shape)
    return (y_prompt, y_sample) + st_p + st_s
```

```python
import functools

import jax
import jax.numpy as jnp
from jax import lax
from jax.experimental import pallas as pl
from jax.experimental.pallas import tpu as pltpu

F32 = jnp.float32
BF16 = jnp.bfloat16
U32 = jnp.uint32

D_MODEL = 2048
HALF_D = D_MODEL // 2
CONV_WIDTH = 4
RMS_EPS = 1e-6
SSM_HEADS = 32
SSM_HEAD_DIM = 64
SSM_GROUPS = 4
SSM_HEADS_PER_GROUP = 8
SSM_STATE = 128
SSM_BC = 2 * SSM_GROUPS * SSM_STATE
SSM_CONV_DIM = D_MODEL + SSM_BC
LRU_BLOCKS = 16
LRU_BLOCK_WIDTH = 128
LRU_C = 8.0
N_EXPERT_GROUPS = 4
EXPERTS_PER_GROUP = 8
N_EXPERTS = 32
TOP_K = 2
EXPERT_FF = 512
PLE_DIM = 256
LANES = 128
SUBLANES = 8
MXU_WIDTH = 256

OFF_XS = D_MODEL
OFF_BC = 2 * D_MODEL
OFF_DT = D_MODEL + SSM_CONV_DIM
OFF_GY = OFF_DT + SSM_HEADS
IN_DIM = OFF_GY + 4 * D_MODEL
PROJ_A_TAIL = 1280
PROJ_A_W = D_MODEL + PROJ_A_TAIL
PROJ_A_TN = PROJ_A_W // 2
A_SEC_Z = 0
A_BLK_BC = D_MODEL // SSM_BC
A_BLK_DT = (D_MODEL + SSM_BC) // LANES
B_SEC_GY, B_SEC_XL, B_SEC_GM, B_SEC_GR = range(4)
G_SEC_GY, G_SEC_GM, G_SEC_GR = range(3)

SSD_CHUNK = 128
MOE_ROWS = 512
VMEM_LIMIT = 56 * 1024 * 1024


def _cparams(sem):
    return pltpu.CompilerParams(dimension_semantics=sem, vmem_limit_bytes=VMEM_LIMIT)


def _rms(x, g):
    return x * lax.rsqrt(jnp.mean(x * x, axis=-1, keepdims=True) + RMS_EPS) * g


def _softplus(x):
    return jnp.maximum(x, 0.0) + jnp.log1p(jnp.exp(-jnp.abs(x)))


def _silu(x):
    return x * jax.nn.sigmoid(x)


def _gelu_tanh(x):
    c = 0.7978845608028654
    return x * (0.5 + 0.5 * jnp.tanh(x * (c + (c * 0.044715) * (x * x))))


def _split3(v):
    hi = v.astype(BF16)
    r1 = v - hi.astype(F32)
    mid = r1.astype(BF16)
    lo = (r1 - mid.astype(F32)).astype(BF16)
    return hi, mid, lo


def _dot(a, b):
    return jnp.dot(a, b, preferred_element_type=F32)


def _dot_exact_rhs(v, m_bf16):
    hi, mid, lo = _split3(v)
    return _dot(hi, m_bf16) + _dot(mid, m_bf16) + _dot(lo, m_bf16)


def _dot_exact_lhs(m_bf16, v):
    hi, mid, lo = _split3(v)
    return _dot(m_bf16, hi) + _dot(m_bf16, mid) + _dot(m_bf16, lo)


def _proj_a_kernel(x_ref, g_ref, w_ref, o_ref, xn_ref):
    xn = _rms(x_ref[...], g_ref[...]).astype(BF16)
    xn_ref[...] = xn
    for n0 in range(0, PROJ_A_W, PROJ_A_TN):
        o_ref[:, n0:n0 + PROJ_A_TN] = _dot(xn, w_ref[:, n0:n0 + PROJ_A_TN])


def _proj_a(x2d, g_mix, w_a, tm):
    t = x2d.shape[0]
    return pl.pallas_call(
        _proj_a_kernel,
        grid=(t // tm,),
        in_specs=[
            pl.BlockSpec((tm, D_MODEL), lambda i: (i, 0)),
            pl.BlockSpec((1, D_MODEL), lambda i: (0, 0)),
            pl.BlockSpec((D_MODEL, PROJ_A_W), lambda i: (0, 0), pipeline_mode=pl.Buffered(1)),
        ],
        out_specs=[
            pl.BlockSpec((tm, PROJ_A_W), lambda i: (i, 0)),
            pl.BlockSpec((tm, D_MODEL), lambda i: (i, 0)),
        ],
        out_shape=[
            jax.ShapeDtypeStruct((t, PROJ_A_W), F32),
            jax.ShapeDtypeStruct((t, D_MODEL), BF16),
        ],
        compiler_params=_cparams(("arbitrary",)),
        name="proj_a",
    )(x2d, g_mix, w_a)


def _proj_gates_kernel(xn_ref, w_ref, o_ref):
    j = pl.program_id(0)
    xn = xn_ref[...]

    def run(act):
        for n0 in range(0, D_MODEL, MXU_WIDTH):
            o_ref[:, n0:n0 + MXU_WIDTH] = act(_dot(xn, w_ref[:, n0:n0 + MXU_WIDTH]))

    @pl.when(j == G_SEC_GY)
    def _():
        run(_gelu_tanh)

    @pl.when(j != G_SEC_GY)
    def _():
        run(jax.nn.sigmoid)


def _proj_gates(xn2d, w_b, tm):
    t = xn2d.shape[0]
    skip_xl = lambda j: jnp.where(j >= B_SEC_XL, j + 1, j)
    return pl.pallas_call(
        _proj_gates_kernel,
        grid=(3, t // tm),
        in_specs=[
            pl.BlockSpec((tm, D_MODEL), lambda j, i: (i, 0)),
            pl.BlockSpec((D_MODEL, D_MODEL), lambda j, i: (0, skip_xl(j))),
        ],
        out_specs=pl.BlockSpec((tm, D_MODEL), lambda j, i: (i, j)),
        out_shape=jax.ShapeDtypeStruct((t, 3 * D_MODEL), F32),
        compiler_params=_cparams(("arbitrary", "arbitrary")),
        name="proj_gates",
    )(xn2d, w_b)


def _conv_tile(cbuf, lt, c0, c1, cw_ref, cb_ref):
    acc = cb_ref[:, c0:c1]
    for k in range(CONV_WIDTH):
        acc = acc + cbuf[SUBLANES - (CONV_WIDTH - 1) + k:SUBLANES - (CONV_WIDTH - 1) + k + lt, c0:c1] * cw_ref[k:k + 1, c0:c1]
    return acc


def _ssd_kernel(xn_ref, xn_nxt_ref, wxs_ref, bc_ref, dt_ref, z_ref, s0_ref, c0_ref, cw_ref, cb_ref, dtb_ref, alog_ref,
                dskip_ref, gssm_ref, rexp_ref, tril_ref,
                y_ref, snew_ref, cnew_ref,
                cbuf, xs_nxt, xact, bcact, yscr, state, *, lt):
    q = SSD_CHUNK
    lp = max(lt, q)
    l = pl.program_id(1)

    @pl.when(l == 0)
    def _():
        cbuf[0:SUBLANES, :] = c0_ref[...]
        state[...] = s0_ref[...].T

    @pl.when(jnp.logical_and(pl.program_id(0) == 0, l == 0))
    def _():
        xs_nxt[...] = _dot(xn_ref[...], wxs_ref[...])

    cbuf[SUBLANES:SUBLANES + lt, 0:D_MODEL] = xs_nxt[...]
    xn_nxt = xn_nxt_ref[...]
    n_slices = (lp // q) * SSM_GROUPS
    slice_w = D_MODEL // n_slices
    cbuf[SUBLANES:SUBLANES + lt, D_MODEL:SSM_CONV_DIM] = bc_ref[...]
    cw = 512
    for c0 in range(0, D_MODEL, cw):
        xact[:, c0:c0 + cw] = _silu(_conv_tile(cbuf, lt, c0, c0 + cw, cw_ref, cb_ref))
    for c0 in range(0, SSM_BC, cw):
        bcact[:, c0:c0 + cw] = _silu(_conv_tile(cbuf, lt, D_MODEL + c0, D_MODEL + c0 + cw, cw_ref, cb_ref))
    tail = cbuf[lt:lt + SUBLANES, :]
    cbuf[0:SUBLANES, :] = tail
    cnew_ref[...] = tail

    dt = _softplus(dt_ref[...] + dtb_ref[...])
    a_neg = -jnp.exp(alog_ref[...])
    da = dt * a_neg
    rexp = rexp_ref[...]
    dt_rep = _dot_exact_rhs(dt, rexp)
    tril = tril_ref[...]
    row_i = lax.broadcasted_iota(jnp.int32, (q, q), 0)
    col_i = lax.broadcasted_iota(jnp.int32, (q, q), 1)
    causal = row_i >= col_i
    lane_i = lax.broadcasted_iota(jnp.int32, (2 * q, LANES), 1)
    row2_i = lax.broadcasted_iota(jnp.int32, (2 * q, LANES), 0)
    pair_mask = (lane_i < SSM_HEAD_DIM) == (row2_i < q)

    def pad_rows(v):
        if lt == lp:
            return v
        return jnp.concatenate([v, jnp.zeros((lp - lt, v.shape[1]), v.dtype)], axis=0)

    for c in range(lp // q):
        r0 = c * q
        r1 = min(r0 + q, lt)
        da_c = pad_rows(da[r0:r1])
        acum = _dot_exact_lhs(tril, da_c)
        acum_t = acum.T
        acum_rep = _dot_exact_rhs(acum, rexp)
        last = acum_rep[q - 1:q, :]
        exp_a = jnp.exp(acum_rep)
        dec_out = jnp.exp(last - acum_rep)
        chunk_dec = jnp.exp(last)
        xa = pad_rows(xact[r0:r1, :])
        xdt = xa * pad_rows(dt_rep[r0:r1])
        xw = (xdt * dec_out).astype(BF16)
        xdt_b = xdt.astype(BF16)
        bca = pad_rows(bcact[r0:r1, :])
        for g in range(SSM_GROUPS):
            s0 = (c * SSM_GROUPS + g) * slice_w
            xs_nxt[:, s0:s0 + slice_w] = _dot(xn_nxt, wxs_ref[:, s0:s0 + slice_w])
            gw = SSM_HEADS_PER_GROUP * SSM_HEAD_DIM
            bg = bca[:, g * SSM_STATE:(g + 1) * SSM_STATE]
            cg = bca[:, SSM_GROUPS * SSM_STATE + g * SSM_STATE:SSM_GROUPS * SSM_STATE + (g + 1) * SSM_STATE]
            bg_b = bg.astype(BF16)
            cg_b = cg.astype(BF16)
            cb = lax.dot_general(cg_b, bg_b, (((1,), (1,)), ((), ())), preferred_element_type=F32)
            sg = state[:, g * gw:(g + 1) * gw]
            y_off = _dot(cg_b, sg.astype(BF16)) * exp_a[:, g * gw:(g + 1) * gw]
            for j in range(SSM_HEADS_PER_GROUP // 2):
                h0 = g * SSM_HEADS_PER_GROUP + 2 * j
                ms = []
                for h in (h0, h0 + 1):
                    seg = acum[:, h:h + 1] - acum_t[h:h + 1, :]
                    ms.append(cb * jnp.exp(jnp.where(causal, seg, -jnp.inf)))
                m = jnp.concatenate(ms, axis=1).astype(BF16)
                xp = xdt_b[:, h0 * SSM_HEAD_DIM:h0 * SSM_HEAD_DIM + LANES]
                rhs = jnp.where(pair_mask, jnp.concatenate([xp, xp], axis=0), jnp.zeros_like(xp[:1, :1]))
                yd = _dot(m, rhs)
                co = h0 * SSM_HEAD_DIM
                yv = yd + y_off[:, j * LANES:(j + 1) * LANES] + dskip_ref[:, co:co + LANES] * xa[:, co:co + LANES]
                yscr[r0:r1, co:co + LANES] = yv[0:r1 - r0]
            state[:, g * gw:(g + 1) * gw] = sg * chunk_dec[:, g * gw:(g + 1) * gw] + _dot(
                bg.T.astype(BF16), xw[:, g * gw:(g + 1) * gw])

    yz = yscr[...] * _silu(z_ref[...])
    y_ref[...] = _rms(yz, gssm_ref[...]).astype(BF16)

    @pl.when(l == pl.num_programs(1) - 1)
    def _():
        snew_ref[...] = state[...].T


def _ssd(xn3, w_xs, proj3, s0, c0p, cw8, cb, dtb, alog, dskip_rep, gssm, rexp, tril, lt):
    b, L, _ = proj3.shape
    nl = L // lt
    kern = functools.partial(_ssd_kernel, lt=lt)
    const = lambda bi, li: (0, 0)
    return pl.pallas_call(
        kern,
        grid=(b, nl),
        in_specs=[
            pl.BlockSpec((None, lt, D_MODEL), lambda bi, li: (bi, li, 0)),
            pl.BlockSpec((None, lt, D_MODEL), _next_tile(b, nl)),
            pl.BlockSpec((D_MODEL, D_MODEL), const, pipeline_mode=pl.Buffered(1)),
            pl.BlockSpec((None, lt, SSM_BC), lambda bi, li: (bi, li, A_BLK_BC)),
            pl.BlockSpec((None, lt, LANES), lambda bi, li: (bi, li, A_BLK_DT)),
            pl.BlockSpec((None, lt, D_MODEL), lambda bi, li: (bi, li, A_SEC_Z)),
            pl.BlockSpec((None, D_MODEL, SSM_STATE), lambda bi, li: (bi, 0, 0)),
            pl.BlockSpec((None, SUBLANES, SSM_CONV_DIM), lambda bi, li: (bi, 0, 0)),
            pl.BlockSpec((SUBLANES, SSM_CONV_DIM), const),
            pl.BlockSpec((1, SSM_CONV_DIM), const),
            pl.BlockSpec((1, LANES), const),
            pl.BlockSpec((1, LANES), const),
            pl.BlockSpec((1, D_MODEL), const),
            pl.BlockSpec((1, D_MODEL), const),
            pl.BlockSpec((LANES, D_MODEL), const),
            pl.BlockSpec((SSD_CHUNK, SSD_CHUNK), const),
        ],
        out_specs=[
            pl.BlockSpec((None, lt, D_MODEL), lambda bi, li: (bi, li, 0)),
            pl.BlockSpec((None, D_MODEL, SSM_STATE), lambda bi, li: (bi, 0, 0)),
            pl.BlockSpec((None, SUBLANES, SSM_CONV_DIM), lambda bi, li: (bi, 0, 0)),
        ],
        out_shape=[
            jax.ShapeDtypeStruct((b, L, D_MODEL), BF16),
            jax.ShapeDtypeStruct((b, D_MODEL, SSM_STATE), F32),
            jax.ShapeDtypeStruct((b, SUBLANES, SSM_CONV_DIM), F32),
        ],
        scratch_shapes=[
            pltpu.VMEM((lt + SUBLANES, SSM_CONV_DIM), F32),
            pltpu.VMEM((lt, D_MODEL), F32),
            pltpu.VMEM((lt, D_MODEL), F32),
            pltpu.VMEM((lt, SSM_BC), F32),
            pltpu.VMEM((lt, D_MODEL), F32),
            pltpu.VMEM((SSM_STATE, D_MODEL), F32),
        ],
        compiler_params=_cparams(("arbitrary", "arbitrary")),
        name="ssd",
    )(xn3, xn3, w_xs, proj3, proj3, proj3, s0, c0p, cw8, cb, dtb, alog, dskip_rep, gssm, rexp, tril)


def _scan_rows(a, u, h_in):
    nv = a.shape[0] // SUBLANES
    a3 = a.reshape(nv, SUBLANES, LANES)
    u3 = u.reshape(nv, SUBLANES, LANES)
    sub = lax.broadcasted_iota(jnp.int32, a3.shape, 1)
    d = 1
    while d < SUBLANES:
        keep = sub >= d
        a_sh = jnp.where(keep, pltpu.roll(a3, d, 1), 1.0)
        u_sh = jnp.where(keep, pltpu.roll(u3, d, 1), 0.0)
        u3 = a3 * u_sh + u3
        a3 = a3 * a_sh
        d *= 2
    hs = []
    h = h_in
    for v in range(nv):
        hv = u3[v] + a3[v] * h
        hs.append(hv)
        h = hv[SUBLANES - 1:SUBLANES, :]
    return jnp.concatenate(hs, axis=0), h


def _lru_kernel(xn_ref, xn_nxt_ref, wxl_ref, ggy_ref, h0_ref, c0_ref, cw_ref, cb_ref, wax_ref, ba_ref, bx_ref,
                lam_ref, hg_ref, hnew_ref, cnew_ref,
                cbuf0, cbuf1, hcar, *, lt):
    l = pl.program_id(1)
    step = pl.program_id(0) * pl.num_programs(1) + l

    @pl.when(step == 0)
    def _():
        cbuf0[SUBLANES:SUBLANES + lt, :] = _dot(xn_ref[...], wxl_ref[...])

    for slot, (cur, nxt) in enumerate(((cbuf0, cbuf1), (cbuf1, cbuf0))):
        @pl.when(step % 2 == slot)
        def _():
            _lru_tile(cur, nxt, l, xn_nxt_ref, wxl_ref, ggy_ref, h0_ref, c0_ref, cw_ref, cb_ref, wax_ref, ba_ref,
                      bx_ref, lam_ref, hg_ref, hnew_ref, cnew_ref, hcar, lt)


def _lru_tile(cbuf, cbuf_nxt, l, xn_nxt_ref, wxl_ref, ggy_ref, h0_ref, c0_ref, cw_ref, cb_ref, wax_ref, ba_ref,
              bx_ref, lam_ref, hg_ref, hnew_ref, cnew_ref, hcar, lt):
    @pl.when(l == 0)
    def _():
        cbuf[0:SUBLANES, :] = c0_ref[...]
        hcar[...] = h0_ref[...]

    xn_nxt = xn_nxt_ref[...]
    for n in range(LRU_BLOCKS):
        c0 = n * LRU_BLOCK_WIDTH
        c1 = c0 + LRU_BLOCK_WIDTH
        if c0 % MXU_WIDTH == 0:
            cbuf_nxt[SUBLANES:SUBLANES + lt, c0:c0 + MXU_WIDTH] = _dot(xn_nxt, wxl_ref[:, c0:c0 + MXU_WIDTH])
        xc = _conv_tile(cbuf, lt, c0, c1, cw_ref, cb_ref)
        pre = _dot(xc.astype(BF16), wax_ref[n])
        r = jax.nn.sigmoid(pre[:, 0:LRU_BLOCK_WIDTH] + ba_ref[:, c0:c1])
        i = jax.nn.sigmoid(pre[:, LRU_BLOCK_WIDTH:2 * LRU_BLOCK_WIDTH] + bx_ref[:, c0:c1])
        log_a = r * ((-LRU_C) * _softplus(-lam_ref[:, c0:c1]))
        a = jnp.exp(log_a)
        t = jnp.tanh(-log_a) * (a * a + 1.0)
        u = jnp.where(t > 0.0, t * lax.rsqrt(t), 0.0) * (i * xc)
        h, h_last = _scan_rows(a, u, hcar[:, c0:c1])
        hg_ref[:, c0:c1] = (h * ggy_ref[:, c0:c1]).astype(BF16)
        hcar[:, c0:c1] = h_last
        hnew_ref[:, c0:c1] = h_last
    tail = cbuf[lt:lt + SUBLANES, :]
    cbuf_nxt[0:SUBLANES, :] = tail
    cnew_ref[...] = tail


def _next_tile(b, nl):
    def index(bi, li, *_):
        nxt = jnp.minimum(bi * nl + li + 1, b * nl - 1)
        return nxt // nl, nxt % nl, 0
    return index


def _lru(xn3, w_b, gates3, h0, c0p, cw8, cb, wax, ba, bx, lam, lt):
    b, L, _ = xn3.shape
    nl = L // lt
    kern = functools.partial(_lru_kernel, lt=lt)
    const = lambda bi, li: (0, 0)
    return pl.pallas_call(
        kern,
        grid=(b, nl),
        in_specs=[
            pl.BlockSpec((None, lt, D_MODEL), lambda bi, li: (bi, li, 0)),
            pl.BlockSpec((None, lt, D_MODEL), _next_tile(b, nl)),
            pl.BlockSpec((D_MODEL, D_MODEL), lambda bi, li: (0, B_SEC_XL), pipeline_mode=pl.Buffered(1)),
            pl.BlockSpec((None, lt, D_MODEL), lambda bi, li: (bi, li, G_SEC_GY)),
            pl.BlockSpec((None, 1, D_MODEL), lambda bi, li: (bi, 0, 0)),
            pl.BlockSpec((None, SUBLANES, D_MODEL), lambda bi, li: (bi, 0, 0)),
            pl.BlockSpec((SUBLANES, D_MODEL), const),
            pl.BlockSpec((1, D_MODEL), const),
            pl.BlockSpec((LRU_BLOCKS, LRU_BLOCK_WIDTH, 2 * LRU_BLOCK_WIDTH), lambda bi, li: (0, 0, 0)),
            pl.BlockSpec((1, D_MODEL), const),
            pl.BlockSpec((1, D_MODEL), const),
            pl.BlockSpec((1, D_MODEL), const),
        ],
        out_specs=[
            pl.BlockSpec((None, lt, D_MODEL), lambda bi, li: (bi, li, 0)),
            pl.BlockSpec((None, 1, D_MODEL), lambda bi, li: (bi, 0, 0)),
            pl.BlockSpec((None, SUBLANES, D_MODEL), lambda bi, li: (bi, 0, 0)),
        ],
        out_shape=[
            jax.ShapeDtypeStruct((b, L, D_MODEL), BF16),
            jax.ShapeDtypeStruct((b, 1, D_MODEL), F32),
            jax.ShapeDtypeStruct((b, SUBLANES, D_MODEL), F32),
        ],
        scratch_shapes=[
            pltpu.VMEM((lt + SUBLANES, D_MODEL), F32),
            pltpu.VMEM((lt + SUBLANES, D_MODEL), F32),
            pltpu.VMEM((1, D_MODEL), F32),
        ],
        compiler_params=_cparams(("arbitrary", "arbitrary")),
        name="lru",
    )(xn3, xn3, w_b, gates3, h0, c0p, cw8, cb, wax, ba, bx, lam)


def _route(logits):
    neg = -1e30
    lane = lax.broadcasted_iota(jnp.int32, logits.shape, 1).astype(F32)
    is_g = lane < N_EXPERT_GROUPS
    gl = jnp.where(is_g, logits, neg)
    gmax = jnp.max(gl, axis=1, keepdims=True)
    gsum = jnp.sum(jnp.where(is_g, jnp.exp(gl - gmax), 0.0), axis=1, keepdims=True)
    g_w = 1.0 / gsum
    gidx = jnp.min(jnp.where(gl == gmax, lane, float(LANES)), axis=1, keepdims=True)
    lo = N_EXPERT_GROUPS + EXPERTS_PER_GROUP * gidx
    el = jnp.where((lane >= lo) & (lane < lo + EXPERTS_PER_GROUP), logits, neg)
    m1 = jnp.max(el, axis=1, keepdims=True)
    i1 = jnp.min(jnp.where(el == m1, lane, float(LANES)), axis=1, keepdims=True)
    el2 = jnp.where(lane == i1, neg, el)
    m2 = jnp.max(el2, axis=1, keepdims=True)
    i2 = jnp.min(jnp.where(el2 == m2, lane, float(LANES)), axis=1, keepdims=True)
    e2 = jnp.exp(m2 - m1)
    w1 = g_w / (1.0 + e2)
    w2 = g_w * e2 / (1.0 + e2)
    out = jnp.where(lane == 0, i1 - N_EXPERT_GROUPS, 0.0)
    out = jnp.where(lane == 1, i2 - N_EXPERT_GROUPS, out)
    out = jnp.where(lane == 2, w1, out)
    out = jnp.where(lane == 3, w2, out)
    return out


def _merge_kernel(x_ref, y_ref, hg_ref, gm_ref, gr_ref, wom_ref, wor_ref, wo_ref, gffn_ref, wr_ref, br_ref,
                  *rest, n_real):
    x1_ref, xn2_ref, route_ref = rest[-3:]
    i = pl.program_id(0)

    @pl.when(i < n_real)
    def _():
        o_m = _dot(y_ref[...], wom_ref[...])
        o_r = _dot(hg_ref[...], wor_ref[...])
        mixed = gm_ref[...] * o_m + gr_ref[...] * o_r
        x1 = x_ref[...] + _dot(mixed.astype(BF16), wo_ref[...])
        x1_ref[...] = x1
        xn2 = _rms(x1, gffn_ref[...])
        x_hi = xn2.astype(BF16)
        x_hi32 = x_hi.astype(F32)
        bits = lax.bitcast_convert_type(x_hi32, U32)
        xn2_ref[...] = (bits[:, :HALF_D] >> 16) | (bits[:, HALF_D:] & jnp.uint32(0xFFFF0000))
        x_lo = (xn2 - x_hi32).astype(BF16)
        logits = _dot(x_hi, wr_ref[0]) + _dot(x_lo, wr_ref[0]) + _dot(x_hi, wr_ref[1]) + br_ref[...]
        route_ref[...] = _route(logits)

    @pl.when(i >= n_real)
    def _():
        xn2_ref[...] = jnp.zeros_like(xn2_ref)
        route_ref[...] = jnp.zeros_like(route_ref)


def _merge(x2d, y2d, hg2d, proj2d, wom, wor, wo, gffn, wr, br, tm, t_all, row_off, shared):
    t = x2d.shape[0]
    n_real = t // tm
    off = row_off // tm
    n_extra = 0
    if shared is None and t_all > t:
        assert row_off == 0 and t_all - t <= tm
        n_extra = 1
    const = lambda i: (0, 0)
    cl = lambda i: jnp.minimum(i, n_real - 1)
    wspec = pl.BlockSpec((D_MODEL, D_MODEL), const, pipeline_mode=pl.Buffered(1))
    in_specs = [
        pl.BlockSpec((tm, D_MODEL), lambda i: (cl(i), 0)),
        pl.BlockSpec((tm, D_MODEL), lambda i: (cl(i), 0)),
        pl.BlockSpec((tm, D_MODEL), lambda i: (cl(i), 0)),
        pl.BlockSpec((tm, D_MODEL), lambda i: (cl(i), G_SEC_GM)),
        pl.BlockSpec((tm, D_MODEL), lambda i: (cl(i), G_SEC_GR)),
        wspec, wspec, wspec,
        pl.BlockSpec((1, D_MODEL), const),
        pl.BlockSpec((2, D_MODEL, LANES), lambda i: (0, 0, 0)),
        pl.BlockSpec((1, LANES), const),
    ]
    args = [x2d, y2d, hg2d, proj2d, proj2d, wom, wor, wo, gffn, wr, br]
    aliases = {}
    if shared is not None:
        in_specs += [pl.BlockSpec(memory_space=pl.ANY), pl.BlockSpec(memory_space=pl.ANY)]
        aliases = {len(args): 1, len(args) + 1: 2}
        args += list(shared)
    return pl.pallas_call(
        functools.partial(_merge_kernel, n_real=n_real),
        grid=(n_real + n_extra,),
        in_specs=in_specs,
        out_specs=[
            pl.BlockSpec((tm, D_MODEL), lambda i: (cl(i), 0)),
            pl.BlockSpec((tm, HALF_D), lambda i: (i + off, 0)),
            pl.BlockSpec((tm, LANES), lambda i: (i + off, 0)),
        ],
        out_shape=[
            jax.ShapeDtypeStruct((t, D_MODEL), F32),
            jax.ShapeDtypeStruct((t_all, HALF_D), U32),
            jax.ShapeDtypeStruct((t_all, LANES), F32),
        ],
        input_output_aliases=aliases,
        compiler_params=_cparams(("arbitrary",)),
        name="merge",
    )(*args)


def _moe_kernel(be_ref, nused_ref, tok_cur_ref, tok_nxt_ref, xn2_hbm, wg_ref, wu_ref, wd_ref, ys_ref,
                xbuf0, xbuf1, wgb, wub, wdb, sem):
    i = pl.program_id(0)
    n_used = nused_ref[0]
    xbufs = (xbuf0, xbuf1)

    def start_row(tok_ref, slot, r):
        pltpu.make_async_copy(xn2_hbm.at[pl.ds(tok_ref[0, r], 1), :], xbufs[slot].at[pl.ds(r, 1), :],
                              sem.at[slot]).start()

    def compute(slot, prefetch):
        pltpu.make_async_copy(xn2_hbm.at[pl.ds(0, MOE_ROWS), :], xbufs[slot], sem.at[slot]).wait()
        words = xbufs[slot][...]
        xa = lax.bitcast_convert_type(words << 16, F32).astype(BF16)
        xb = lax.bitcast_convert_type(words & jnp.uint32(0xFFFF0000), F32).astype(BF16)
        third = MOE_ROWS // 3

        def prefetch_rows(r0, r1):
            if prefetch:
                for r in range(r0, r1):
                    start_row(tok_nxt_ref, 1 - slot, r)

        prefetch_rows(0, third)
        hg = _dot(xa, wgb[0:HALF_D, :]) + _dot(xb, wgb[HALF_D:, :])
        prefetch_rows(third, 2 * third)
        hu = _dot(xa, wub[0:HALF_D, :]) + _dot(xb, wub[HALF_D:, :])
        prefetch_rows(2 * third, MOE_ROWS)
        ys_ref[...] = _dot((_silu(hg) * hu).astype(BF16), wdb[...])

    fresh = jnp.logical_or(i == 0, be_ref[i] != be_ref[jnp.maximum(i - 1, 0)])

    @pl.when(jnp.logical_and(fresh, i < n_used))
    def _():
        wgb[...] = wg_ref[...].astype(BF16)
        wub[...] = wu_ref[...].astype(BF16)
        wdb[...] = wd_ref[...].astype(BF16)

    @pl.when(jnp.logical_and(i == 0, n_used > 0))
    def _():
        def issue(r, c):
            start_row(tok_cur_ref, 0, r)
            return c
        lax.fori_loop(0, MOE_ROWS, issue, 0)

    for slot in (0, 1):
        mine = i % 2 == slot

        @pl.when(jnp.logical_and(mine, i + 1 < n_used))
        def _():
            compute(slot, True)

        @pl.when(jnp.logical_and(mine, jnp.logical_and(i < n_used, i + 1 >= n_used)))
        def _():
            compute(slot, False)

    @pl.when(i >= n_used)
    def _():
        ys_ref[...] = jnp.zeros_like(ys_ref)


def _moe(block_expert, n_used, row_token3, xn2_all, wg, wu, wd):
    n_blocks = row_token3.shape[0]
    grid_spec = pltpu.PrefetchScalarGridSpec(
        num_scalar_prefetch=2,
        grid=(n_blocks,),
        in_specs=[
            pl.BlockSpec((None, 1, MOE_ROWS), lambda i, be, nu: (i, 0, 0), memory_space=pltpu.SMEM),
            pl.BlockSpec((None, 1, MOE_ROWS), lambda i, be, nu: (jnp.minimum(i + 1, n_blocks - 1), 0, 0),
                         memory_space=pltpu.SMEM),
            pl.BlockSpec(memory_space=pl.ANY),
            pl.BlockSpec((None, D_MODEL, EXPERT_FF), lambda i, be, nu: (be[i], 0, 0)),
            pl.BlockSpec((None, D_MODEL, EXPERT_FF), lambda i, be, nu: (be[i], 0, 0)),
            pl.BlockSpec((None, EXPERT_FF, D_MODEL), lambda i, be, nu: (be[i], 0, 0)),
        ],
        out_specs=pl.BlockSpec((MOE_ROWS, D_MODEL), lambda i, be, nu: (i, 0)),
        scratch_shapes=[
            pltpu.VMEM((MOE_ROWS, HALF_D), U32),
            pltpu.VMEM((MOE_ROWS, HALF_D), U32),
            pltpu.VMEM((D_MODEL, EXPERT_FF), BF16),
            pltpu.VMEM((D_MODEL, EXPERT_FF), BF16),
            pltpu.VMEM((EXPERT_FF, D_MODEL), BF16),
            pltpu.SemaphoreType.DMA((2,)),
        ],
    )
    return pl.pallas_call(
        _moe_kernel,
        grid_spec=grid_spec,
        out_shape=jax.ShapeDtypeStruct((n_blocks * MOE_ROWS, D_MODEL), F32),
        compiler_params=_cparams(("arbitrary",)),
        name="moe",
    )(block_expert, n_used, row_token3, row_token3, xn2_all, wg, wu, wd)


def _final_kernel(d_cur_ref, d_nxt_ref, ys_hbm, x1_ref, route_ref, p_ref, gple_ref, wpg_ref, wple_ref, gfin_ref,
                  o_ref, gbuf0, gbuf1, sem, *, tm):
    i = pl.program_id(0)
    n = pl.num_programs(0)
    gbufs = (gbuf0, gbuf1)

    def start_row(d_ref, slot, r):
        pltpu.make_async_copy(ys_hbm.at[pl.ds(d_ref[0, r], 1), :], gbufs[slot].at[pl.ds(r, 1), :],
                              sem.at[slot]).start()

    def compute(slot, prefetch):
        gbuf = gbufs[slot]
        pltpu.make_async_copy(ys_hbm.at[pl.ds(0, TOP_K * tm), :], gbuf, sem.at[slot]).wait()
        route = route_ref[...]
        moe = route[:, 2:3] * gbuf[0:tm, :] + route[:, 3:4] * gbuf[tm:2 * tm, :]
        x2 = x1_ref[...] + moe
        xn3 = _rms(x2, gple_ref[...]).astype(BF16)
        if prefetch:
            for r in range(TOP_K * tm):
                start_row(d_nxt_ref, 1 - slot, r)
        gate = jax.nn.sigmoid(_dot(xn3, wpg_ref[...]))
        x3 = x2 + gate * _dot(p_ref[...].astype(BF16), wple_ref[...])
        o_ref[...] = _rms(x3, gfin_ref[...])

    @pl.when(i == 0)
    def _():
        def issue(r, c):
            start_row(d_cur_ref, 0, r)
            return c
        lax.fori_loop(0, TOP_K * tm, issue, 0)

    for slot in (0, 1):
        mine = i % 2 == slot

        @pl.when(jnp.logical_and(mine, i + 1 < n))
        def _():
            compute(slot, True)

        @pl.when(jnp.logical_and(mine, i + 1 >= n))
        def _():
            compute(slot, False)


def _final(dest3, ys, x1, route_all, p2d, gple, wpg, wple, gfin, tm, row_off):
    t = x1.shape[0]
    n = t // tm
    off = row_off // tm
    const = lambda i: (0, 0)
    kern = functools.partial(_final_kernel, tm=tm)
    return pl.pallas_call(
        kern,
        grid=(n,),
        in_specs=[
            pl.BlockSpec((None, 1, TOP_K * tm), lambda i: (i, 0, 0), memory_space=pltpu.SMEM),
            pl.BlockSpec((None, 1, TOP_K * tm), lambda i: (jnp.minimum(i + 1, n - 1), 0, 0), memory_space=pltpu.SMEM),
            pl.BlockSpec(memory_space=pl.ANY),
            pl.BlockSpec((tm, D_MODEL), lambda i: (i, 0)),
            pl.BlockSpec((tm, LANES), lambda i: (i + off, 0)),
            pl.BlockSpec((tm, PLE_DIM), lambda i: (i, 0)),
            pl.BlockSpec((1, D_MODEL), const),
            pl.BlockSpec((D_MODEL, D_MODEL), const, pipeline_mode=pl.Buffered(1)),
            pl.BlockSpec((PLE_DIM, D_MODEL), const),
            pl.BlockSpec((1, D_MODEL), const),
        ],
        out_specs=pl.BlockSpec((tm, D_MODEL), lambda i: (i, 0)),
        out_shape=jax.ShapeDtypeStruct((t, D_MODEL), F32),
        scratch_shapes=[
            pltpu.VMEM((TOP_K * tm, D_MODEL), F32),
            pltpu.VMEM((TOP_K * tm, D_MODEL), F32),
            pltpu.SemaphoreType.DMA((2,)),
        ],
        compiler_params=_cparams(("arbitrary",)),
        name="final",
    )(dest3, dest3, ys, x1, route_all, p2d, gple, wpg, wple, gfin)


def _pad_rows8(a, at_end=False):
    r = a.shape[-2]
    pad = [(0, 0)] * (a.ndim - 2) + ([(0, SUBLANES - r)] if at_end else [(SUBLANES - r, 0)]) + [(0, 0)]
    return jnp.pad(a, pad)


def _dispatch(route_all):
    t_all = route_all.shape[0]
    n_assign = t_all * TOP_K
    n_blocks = -(-n_assign // MOE_ROWS) + N_EXPERTS
    ids = jnp.arange(N_EXPERTS, dtype=jnp.int32)[None, :]
    onehots = [(route_all[:, k].astype(jnp.int32)[:, None] == ids).astype(jnp.int32) for k in range(TOP_K)]
    per_token = onehots[0] + onehots[1]
    csum = jnp.cumsum(per_token, axis=0)
    before = csum - per_token
    counts = csum[-1]
    nblk = (counts + MOE_ROWS - 1) // MOE_ROWS
    blk_end = jnp.cumsum(nblk)
    blk_start = blk_end - nblk
    slot0 = blk_start[None, :] * MOE_ROWS + before
    dest = jnp.stack([jnp.sum(oh * slot0, axis=1) for oh in onehots])
    token = jnp.tile(jnp.arange(t_all, dtype=jnp.int32), TOP_K)
    row_token = jnp.zeros((n_blocks * MOE_ROWS,), jnp.int32).at[dest.reshape(-1)].set(token)
    block_expert = jnp.minimum(jnp.sum(blk_end[None, :] <= jnp.arange(n_blocks, dtype=jnp.int32)[:, None], axis=1),
                               N_EXPERTS - 1).astype(jnp.int32)
    n_used = blk_end[-1:].astype(jnp.int32)
    return dest.astype(jnp.int32), row_token.reshape(n_blocks, 1, MOE_ROWS), block_expert, n_used


def _tile_dest(dest, tm):
    t = dest.shape[1]
    return dest.reshape(TOP_K, t // tm, tm).transpose(1, 0, 2).reshape(t // tm, 1, TOP_K * tm)


def kernel(x_prompt, x_sample, state_ssm, state_ssm_conv, state_lru, state_lru_conv, p_prompt, p_sample, g_mix, w_in, conv_m_w, conv_m_b, dt_bias, a_log, d_skip, g_ssm, w_out_m, conv_r_w, conv_r_b, w_lru_a, b_lru_a, w_lru_x, b_lru_x, lru_lambda, w_out_r, w_o, g_ffn, w_router_group, b_router_group, w_router_expert, b_router_expert, w_exp_gate, w_exp_up, w_exp_down, g_ple, w_ple_gate, w_ple, g_final):
    depth = w_in.shape[0]
    assert depth == 1, "one layer per step"
    bp, lp_, _ = x_prompt.shape
    bs, ls, _ = x_sample.shape
    tp, ts = bp * lp_, bs * ls
    t_all = tp + ts

    w = w_in[0]
    assert w.shape == (D_MODEL, IN_DIM)
    w_a = jnp.concatenate([w[:, 0:OFF_XS], w[:, OFF_BC:OFF_BC + PROJ_A_TAIL]], axis=1).astype(BF16)
    w_xs = w[:, OFF_XS:OFF_BC].astype(BF16)
    w_b = w[:, OFF_GY:IN_DIM].astype(BF16)
    row = lambda v: v.reshape(1, -1).astype(F32)
    pad_lanes = lambda v: jnp.pad(v.reshape(1, -1).astype(F32), ((0, 0), (0, LANES - v.shape[-1])))
    cw_m = _pad_rows8(conv_m_w[0].astype(F32), at_end=True)
    cw_r = _pad_rows8(conv_r_w[0].astype(F32), at_end=True)
    dskip_rep = jnp.repeat(d_skip[0].astype(F32), SSM_HEAD_DIM).reshape(1, D_MODEL)
    rexp = (jnp.arange(D_MODEL)[None, :] // SSM_HEAD_DIM == jnp.arange(LANES)[:, None]).astype(BF16)
    tril = jnp.tril(jnp.ones((SSD_CHUNK, SSD_CHUNK), F32)).astype(BF16)
    wax = jnp.concatenate([w_lru_a[0], w_lru_x[0]], axis=-1).astype(BF16)
    wom, wor, wo = w_out_m[0].astype(BF16), w_out_r[0].astype(BF16), w_o[0].astype(BF16)
    wr32 = jnp.pad(jnp.concatenate([w_router_group[0], w_router_expert[0]], axis=1).astype(F32),
                   ((0, 0), (0, LANES - N_EXPERT_GROUPS - N_EXPERTS)))
    wr_hi = wr32.astype(BF16)
    wr = jnp.stack([wr_hi, (wr32 - wr_hi.astype(F32)).astype(BF16)])
    br = pad_lanes(jnp.concatenate([b_router_group[0], b_router_expert[0]]))
    wg, wu, wd = w_exp_gate[0].astype(F32), w_exp_up[0].astype(F32), w_exp_down[0].astype(F32)
    wpg, wple = w_ple_gate[0].astype(BF16), w_ple[0].astype(BF16)

    def branches(x, ssm0, sconv0, lru0, lconv0, tm_proj, lt):
        b, L, _ = x.shape
        x2d = x.reshape(b * L, D_MODEL)
        proj_a, xn = _proj_a(x2d, row(g_mix[0]), w_a, tm_proj)
        gates = _proj_gates(xn, w_b, min(2 * tm_proj, b * L))
        xn3 = xn.reshape(b, L, D_MODEL)
        y, ssm_new, sconv8 = _ssd(xn3, w_xs, proj_a.reshape(b, L, PROJ_A_W),
                                  ssm0.reshape(b, D_MODEL, SSM_STATE).astype(F32), _pad_rows8(sconv0.astype(F32)),
                                  cw_m, row(conv_m_b[0]), pad_lanes(dt_bias[0]), pad_lanes(a_log[0]), dskip_rep,
                                  row(g_ssm[0]), rexp, tril, lt)
        hg, lru_new, lconv8 = _lru(xn3, w_b, gates.reshape(b, L, 3 * D_MODEL),
                                   lru0.reshape(b, 1, D_MODEL).astype(F32), _pad_rows8(lconv0.astype(F32)),
                                   cw_r, row(conv_r_b[0]), wax, row(b_lru_a[0]), row(b_lru_x[0]), row(lru_lambda[0]), lt)
        states = (ssm_new.reshape(1, b, SSM_HEADS, SSM_HEAD_DIM, SSM_STATE),
                  sconv8[None, :, SUBLANES - (CONV_WIDTH - 1):, :],
                  lru_new.reshape(1, b, D_MODEL),
                  lconv8[None, :, SUBLANES - (CONV_WIDTH - 1):, :])
        return x2d, gates, y.reshape(b * L, D_MODEL), hg.reshape(b * L, D_MODEL), states

    zeros = lambda *s: jnp.zeros(s, F32)
    lt_p = min(256, lp_)
    tm_p = min(256, tp)
    tm_s = min(128, ts)
    xp2, proj_p, y_p, hg_p, st_p = branches(
        x_prompt, zeros(bp, SSM_HEADS, SSM_HEAD_DIM, SSM_STATE), zeros(bp, CONV_WIDTH - 1, SSM_CONV_DIM),
        zeros(bp, D_MODEL), zeros(bp, CONV_WIDTH - 1, D_MODEL), min(512, tp), lt_p)
    xs2, proj_s, y_s, hg_s, st_s = branches(
        x_sample, state_ssm[0], state_ssm_conv[0], state_lru[0], state_lru_conv[0], tm_s, ls)

    x1_p, xn2_all, route_all = _merge(xp2, y_p, hg_p, proj_p, wom, wor, wo, row(g_ffn[0]), wr, br, tm_p, t_all, 0, None)
    x1_s, xn2_all, route_all = _merge(xs2, y_s, hg_s, proj_s, wom, wor, wo, row(g_ffn[0]), wr, br, tm_s, t_all, tp,
                                      (xn2_all, route_all))

    dest, row_token3, block_expert, n_used = _dispatch(route_all)
    ys = _moe(block_expert, n_used, row_token3, xn2_all, wg, wu, wd)

    fin = lambda x1, d, p, tm, off: _final(_tile_dest(d, tm), ys, x1, route_all, p.reshape(-1, PLE_DIM).astype(F32),
                                           row(g_ple[0]), wpg, wple, row(g_final), tm, off)
    y_prompt = fin(x1_p, dest[:, :tp], p_prompt[0], tm_p, 0).reshape(x_prompt.shape)
    y_sample = fin(x1_s, dest[:, tp:], p_sample[0], tm_s, tp).reshape(x_sample.shape)
    return (y_prompt, y_sample) + st_p + st_s
```

```python
import functools

import jax
import jax.numpy as jnp
from jax import lax
from jax.experimental import pallas as pl
from jax.experimental.pallas import tpu as pltpu

F32 = jnp.float32
BF16 = jnp.bfloat16
U32 = jnp.uint32

D_MODEL = 2048
HALF_D = D_MODEL // 2
CONV_WIDTH = 4
RMS_EPS = 1e-6
SSM_HEADS = 32
SSM_HEAD_DIM = 64
SSM_GROUPS = 4
SSM_HEADS_PER_GROUP = 8
SSM_STATE = 128
SSM_BC = 2 * SSM_GROUPS * SSM_STATE
SSM_CONV_DIM = D_MODEL + SSM_BC
LRU_BLOCKS = 16
LRU_BLOCK_WIDTH = 128
LRU_C = 8.0
N_EXPERT_GROUPS = 4
EXPERTS_PER_GROUP = 8
N_EXPERTS = 32
TOP_K = 2
EXPERT_FF = 512
PLE_DIM = 256
LANES = 128
SUBLANES = 8
MXU_WIDTH = 256

OFF_XS = D_MODEL
OFF_BC = 2 * D_MODEL
OFF_DT = D_MODEL + SSM_CONV_DIM
OFF_GY = OFF_DT + SSM_HEADS
IN_DIM = OFF_GY + 4 * D_MODEL
PROJ_A_TAIL = 1280
PROJ_A_W = D_MODEL + PROJ_A_TAIL
PROJ_A_TN = PROJ_A_W // 2
A_SEC_Z = 0
A_BLK_BC = D_MODEL // SSM_BC
A_BLK_DT = (D_MODEL + SSM_BC) // LANES
B_SEC_GY, B_SEC_XL, B_SEC_GM, B_SEC_GR = range(4)
G_SEC_GY, G_SEC_GM, G_SEC_GR = range(3)

SSD_CHUNK = 128
MOE_ROWS = 512
VMEM_LIMIT = 56 * 1024 * 1024


def _cparams(sem):
    return pltpu.CompilerParams(dimension_semantics=sem, vmem_limit_bytes=VMEM_LIMIT)


def _rms(x, g):
    return x * lax.rsqrt(jnp.mean(x * x, axis=-1, keepdims=True) + RMS_EPS) * g


def _softplus(x):
    return jnp.maximum(x, 0.0) + jnp.log1p(jnp.exp(-jnp.abs(x)))


def _silu(x):
    return x * jax.nn.sigmoid(x)


def _gelu_tanh(x):
    c = 0.7978845608028654
    return x * (0.5 + 0.5 * jnp.tanh(x * (c + (c * 0.044715) * (x * x))))


def _split3(v):
    hi = v.astype(BF16)
    r1 = v - hi.astype(F32)
    mid = r1.astype(BF16)
    lo = (r1 - mid.astype(F32)).astype(BF16)
    return hi, mid, lo


def _dot(a, b):
    return jnp.dot(a, b, preferred_element_type=F32)


def _dot_exact_rhs(v, m_bf16):
    hi, mid, lo = _split3(v)
    return _dot(hi, m_bf16) + _dot(mid, m_bf16) + _dot(lo, m_bf16)


def _dot_exact_lhs(m_bf16, v):
    hi, mid, lo = _split3(v)
    return _dot(m_bf16, hi) + _dot(m_bf16, mid) + _dot(m_bf16, lo)


def _proj_a_kernel(x_ref, g_ref, w_ref, o_ref, xn_ref):
    xn = _rms(x_ref[...], g_ref[...]).astype(BF16)
    xn_ref[...] = xn
    for n0 in range(0, PROJ_A_W, PROJ_A_TN):
        o_ref[:, n0:n0 + PROJ_A_TN] = _dot(xn, w_ref[:, n0:n0 + PROJ_A_TN])


def _proj_a(x2d, g_mix, w_a, tm):
    t = x2d.shape[0]
    return pl.pallas_call(
        _proj_a_kernel,
        grid=(t // tm,),
        in_specs=[
            pl.BlockSpec((tm, D_MODEL), lambda i: (i, 0)),
            pl.BlockSpec((1, D_MODEL), lambda i: (0, 0)),
            pl.BlockSpec((D_MODEL, PROJ_A_W), lambda i: (0, 0), pipeline_mode=pl.Buffered(1)),
        ],
        out_specs=[
            pl.BlockSpec((tm, PROJ_A_W), lambda i: (i, 0)),
            pl.BlockSpec((tm, D_MODEL), lambda i: (i, 0)),
        ],
        out_shape=[
            jax.ShapeDtypeStruct((t, PROJ_A_W), F32),
            jax.ShapeDtypeStruct((t, D_MODEL), BF16),
        ],
        compiler_params=_cparams(("arbitrary",)),
        name="proj_a",
    )(x2d, g_mix, w_a)


def _proj_gates_kernel(xn_ref, w_ref, o_ref):
    j = pl.program_id(0)
    xn = xn_ref[...]

    def run(act):
        for n0 in range(0, D_MODEL, MXU_WIDTH):
            o_ref[:, n0:n0 + MXU_WIDTH] = act(_dot(xn, w_ref[:, n0:n0 + MXU_WIDTH]))

    @pl.when(j == G_SEC_GY)
    def _():
        run(_gelu_tanh)

    @pl.when(j != G_SEC_GY)
    def _():
        run(jax.nn.sigmoid)


def _proj_gates(xn2d, w_b, tm):
    t = xn2d.shape[0]
    skip_xl = lambda j: jnp.where(j >= B_SEC_XL, j + 1, j)
    return pl.pallas_call(
        _proj_gates_kernel,
        grid=(3, t // tm),
        in_specs=[
            pl.BlockSpec((tm, D_MODEL), lambda j, i: (i, 0)),
            pl.BlockSpec((D_MODEL, D_MODEL), lambda j, i: (0, skip_xl(j))),
        ],
        out_specs=pl.BlockSpec((tm, D_MODEL), lambda j, i: (i, j)),
        out_shape=jax.ShapeDtypeStruct((t, 3 * D_MODEL), F32),
        compiler_params=_cparams(("arbitrary", "arbitrary")),
        name="proj_gates",
    )(xn2d, w_b)


def _conv_tile(cbuf, lt, c0, c1, cw_ref, cb_ref):
    acc = cb_ref[:, c0:c1]
    for k in range(CONV_WIDTH):
        acc = acc + cbuf[SUBLANES - (CONV_WIDTH - 1) + k:SUBLANES - (CONV_WIDTH - 1) + k + lt, c0:c1] * cw_ref[k:k + 1, c0:c1]
    return acc


def _ssd_kernel(xn_ref, xn_nxt_ref, wxs_ref, bc_ref, dt_ref, z_ref, s0_ref, c0_ref, cw_ref, cb_ref, dtb_ref, alog_ref,
                dskip_ref, gssm_ref, rexp_ref, tril_ref,
                y_ref, snew_ref, cnew_ref,
                cbuf, xs_nxt, xact, bcact, yscr, state, *, lt):
    q = SSD_CHUNK
    lp = max(lt, q)
    l = pl.program_id(1)

    @pl.when(l == 0)
    def _():
        cbuf[0:SUBLANES, :] = c0_ref[...]
        state[...] = s0_ref[...].T

    @pl.when(jnp.logical_and(pl.program_id(0) == 0, l == 0))
    def _():
        xs_nxt[...] = _dot(xn_ref[...], wxs_ref[...])

    cbuf[SUBLANES:SUBLANES + lt, 0:D_MODEL] = xs_nxt[...]
    xn_nxt = xn_nxt_ref[...]
    n_slices = (lp // q) * SSM_GROUPS
    slice_w = D_MODEL // n_slices
    cbuf[SUBLANES:SUBLANES + lt, D_MODEL:SSM_CONV_DIM] = bc_ref[...]
    cw = 512
    for c0 in range(0, D_MODEL, cw):
        xact[:, c0:c0 + cw] = _silu(_conv_tile(cbuf, lt, c0, c0 + cw, cw_ref, cb_ref))
    for c0 in range(0, SSM_BC, cw):
        bcact[:, c0:c0 + cw] = _silu(_conv_tile(cbuf, lt, D_MODEL + c0, D_MODEL + c0 + cw, cw_ref, cb_ref))
    tail = cbuf[lt:lt + SUBLANES, :]
    cbuf[0:SUBLANES, :] = tail
    cnew_ref[...] = tail

    dt = _softplus(dt_ref[...] + dtb_ref[...])
    a_neg = -jnp.exp(alog_ref[...])
    da = dt * a_neg
    rexp = rexp_ref[...]
    dt_rep = _dot_exact_rhs(dt, rexp)
    tril = tril_ref[...]
    row_i = lax.broadcasted_iota(jnp.int32, (q, q), 0)
    col_i = lax.broadcasted_iota(jnp.int32, (q, q), 1)
    causal = row_i >= col_i
    lane_i = lax.broadcasted_iota(jnp.int32, (2 * q, LANES), 1)
    row2_i = lax.broadcasted_iota(jnp.int32, (2 * q, LANES), 0)
    pair_mask = (lane_i < SSM_HEAD_DIM) == (row2_i < q)

    def pad_rows(v):
        if lt == lp:
            return v
        return jnp.concatenate([v, jnp.zeros((lp - lt, v.shape[1]), v.dtype)], axis=0)

    for c in range(lp // q):
        r0 = c * q
        r1 = min(r0 + q, lt)
        da_c = pad_rows(da[r0:r1])
        acum = _dot_exact_lhs(tril, da_c)
        acum_t = acum.T
        acum_rep = _dot_exact_rhs(acum, rexp)
        last = acum_rep[q - 1:q, :]
        exp_a = jnp.exp(acum_rep)
        dec_out = jnp.exp(last - acum_rep)
        chunk_dec = jnp.exp(last)
        xa = pad_rows(xact[r0:r1, :])
        xdt = xa * pad_rows(dt_rep[r0:r1])
        xw = (xdt * dec_out).astype(BF16)
        xdt_b = xdt.astype(BF16)
        bca = pad_rows(bcact[r0:r1, :])
        for g in range(SSM_GROUPS):
            s0 = (c * SSM_GROUPS + g) * slice_w
            xs_nxt[:, s0:s0 + slice_w] = _dot(xn_nxt, wxs_ref[:, s0:s0 + slice_w])
            gw = SSM_HEADS_PER_GROUP * SSM_HEAD_DIM
            bg = bca[:, g * SSM_STATE:(g + 1) * SSM_STATE]
            cg = bca[:, SSM_GROUPS * SSM_STATE + g * SSM_STATE:SSM_GROUPS * SSM_STATE + (g + 1) * SSM_STATE]
            bg_b = bg.astype(BF16)
            cg_b = cg.astype(BF16)
            cb = lax.dot_general(cg_b, bg_b, (((1,), (1,)), ((), ())), preferred_element_type=F32)
            sg = state[:, g * gw:(g + 1) * gw]
            y_off = _dot(cg_b, sg.astype(BF16)) * exp_a[:, g * gw:(g + 1) * gw]
            for j in range(SSM_HEADS_PER_GROUP // 2):
                h0 = g * SSM_HEADS_PER_GROUP + 2 * j
                ms = []
                for h in (h0, h0 + 1):
                    seg = acum[:, h:h + 1] - acum_t[h:h + 1, :]
                    ms.append(cb * jnp.exp(jnp.where(causal, seg, -jnp.inf)))
                m = jnp.concatenate(ms, axis=1).astype(BF16)
                xp = xdt_b[:, h0 * SSM_HEAD_DIM:h0 * SSM_HEAD_DIM + LANES]
                rhs = jnp.where(pair_mask, jnp.concatenate([xp, xp], axis=0), jnp.zeros_like(xp[:1, :1]))
                yd = _dot(m, rhs)
                co = h0 * SSM_HEAD_DIM
                yv = yd + y_off[:, j * LANES:(j + 1) * LANES] + dskip_ref[:, co:co + LANES] * xa[:, co:co + LANES]
                yscr[r0:r1, co:co + LANES] = yv[0:r1 - r0]
            state[:, g * gw:(g + 1) * gw] = sg * chunk_dec[:, g * gw:(g + 1) * gw] + _dot(
                bg.T.astype(BF16), xw[:, g * gw:(g + 1) * gw])

    yz = yscr[...] * _silu(z_ref[...])
    y_ref[...] = _rms(yz, gssm_ref[...]).astype(BF16)

    @pl.when(l == pl.num_programs(1) - 1)
    def _():
        snew_ref[...] = state[...].T


def _ssd(xn3, w_xs, proj3, s0, c0p, cw8, cb, dtb, alog, dskip_rep, gssm, rexp, tril, lt):
    b, L, _ = proj3.shape
    nl = L // lt
    kern = functools.partial(_ssd_kernel, lt=lt)
    const = lambda bi, li: (0, 0)
    return pl.pallas_call(
        kern,
        grid=(b, nl),
        in_specs=[
            pl.BlockSpec((None, lt, D_MODEL), lambda bi, li: (bi, li, 0)),
            pl.BlockSpec((None, lt, D_MODEL), _next_tile(b, nl)),
            pl.BlockSpec((D_MODEL, D_MODEL), const, pipeline_mode=pl.Buffered(1)),
            pl.BlockSpec((None, lt, SSM_BC), lambda bi, li: (bi, li, A_BLK_BC)),
            pl.BlockSpec((None, lt, LANES), lambda bi, li: (bi, li, A_BLK_DT)),
            pl.BlockSpec((None, lt, D_MODEL), lambda bi, li: (bi, li, A_SEC_Z)),
            pl.BlockSpec((None, D_MODEL, SSM_STATE), lambda bi, li: (bi, 0, 0)),
            pl.BlockSpec((None, SUBLANES, SSM_CONV_DIM), lambda bi, li: (bi, 0, 0)),
            pl.BlockSpec((SUBLANES, SSM_CONV_DIM), const),
            pl.BlockSpec((1, SSM_CONV_DIM), const),
            pl.BlockSpec((1, LANES), const),
            pl.BlockSpec((1, LANES), const),
            pl.BlockSpec((1, D_MODEL), const),
            pl.BlockSpec((1, D_MODEL), const),
            pl.BlockSpec((LANES, D_MODEL), const),
            pl.BlockSpec((SSD_CHUNK, SSD_CHUNK), const),
        ],
        out_specs=[
            pl.BlockSpec((None, lt, D_MODEL), lambda bi, li: (bi, li, 0)),
            pl.BlockSpec((None, D_MODEL, SSM_STATE), lambda bi, li: (bi, 0, 0)),
            pl.BlockSpec((None, SUBLANES, SSM_CONV_DIM), lambda bi, li: (bi, 0, 0)),
        ],
        out_shape=[
            jax.ShapeDtypeStruct((b, L, D_MODEL), BF16),
            jax.ShapeDtypeStruct((b, D_MODEL, SSM_STATE), F32),
            jax.ShapeDtypeStruct((b, SUBLANES, SSM_CONV_DIM), F32),
        ],
        scratch_shapes=[
            pltpu.VMEM((lt + SUBLANES, SSM_CONV_DIM), F32),
            pltpu.VMEM((lt, D_MODEL), F32),
            pltpu.VMEM((lt, D_MODEL), F32),
            pltpu.VMEM((lt, SSM_BC), F32),
            pltpu.VMEM((lt, D_MODEL), F32),
            pltpu.VMEM((SSM_STATE, D_MODEL), F32),
        ],
        compiler_params=_cparams(("arbitrary", "arbitrary")),
        name="ssd",
    )(xn3, xn3, w_xs, proj3, proj3, proj3, s0, c0p, cw8, cb, dtb, alog, dskip_rep, gssm, rexp, tril)


def _scan_rows(a, u, h_in):
    nv = a.shape[0] // SUBLANES
    a3 = a.reshape(nv, SUBLANES, LANES)
    u3 = u.reshape(nv, SUBLANES, LANES)
    sub = lax.broadcasted_iota(jnp.int32, a3.shape, 1)
    d = 1
    while d < SUBLANES:
        keep = sub >= d
        a_sh = jnp.where(keep, pltpu.roll(a3, d, 1), 1.0)
        u_sh = jnp.where(keep, pltpu.roll(u3, d, 1), 0.0)
        u3 = a3 * u_sh + u3
        a3 = a3 * a_sh
        d *= 2
    hs = []
    h = h_in
    for v in range(nv):
        hv = u3[v] + a3[v] * h
        hs.append(hv)
        h = hv[SUBLANES - 1:SUBLANES, :]
    return jnp.concatenate(hs, axis=0), h


def _lru_kernel(xn_ref, xn_nxt_ref, wxl_ref, ggy_ref, h0_ref, c0_ref, cw_ref, cb_ref, wax_ref, ba_ref, bx_ref,
                lam_ref, hg_ref, hnew_ref, cnew_ref,
                cbuf, xl_nxt, hcar, *, lt):
    l = pl.program_id(1)

    @pl.when(jnp.logical_and(pl.program_id(0) == 0, l == 0))
    def _():
        xl_nxt[...] = _dot(xn_ref[...], wxl_ref[...])

    @pl.when(l == 0)
    def _():
        cbuf[0:SUBLANES, :] = c0_ref[...]
        hcar[...] = h0_ref[...]

    cbuf[SUBLANES:SUBLANES + lt, :] = xl_nxt[...]
    xn_nxt = xn_nxt_ref[...]
    for n in range(LRU_BLOCKS):
        c0 = n * LRU_BLOCK_WIDTH
        c1 = c0 + LRU_BLOCK_WIDTH
        if c0 % MXU_WIDTH == 0:
            xl_nxt[:, c0:c0 + MXU_WIDTH] = _dot(xn_nxt, wxl_ref[:, c0:c0 + MXU_WIDTH])
        xc = _conv_tile(cbuf, lt, c0, c1, cw_ref, cb_ref)
        pre = _dot(xc.astype(BF16), wax_ref[n])
        r = jax.nn.sigmoid(pre[:, 0:LRU_BLOCK_WIDTH] + ba_ref[:, c0:c1])
        i = jax.nn.sigmoid(pre[:, LRU_BLOCK_WIDTH:2 * LRU_BLOCK_WIDTH] + bx_ref[:, c0:c1])
        log_a = r * ((-LRU_C) * _softplus(-lam_ref[:, c0:c1]))
        a = jnp.exp(log_a)
        t = jnp.tanh(-log_a) * (a * a + 1.0)
        u = jnp.where(t > 0.0, t * lax.rsqrt(t), 0.0) * (i * xc)
        h, h_last = _scan_rows(a, u, hcar[:, c0:c1])
        hg_ref[:, c0:c1] = (h * ggy_ref[:, c0:c1]).astype(BF16)
        hcar[:, c0:c1] = h_last
        hnew_ref[:, c0:c1] = h_last
    tail = cbuf[lt:lt + SUBLANES, :]
    cbuf[0:SUBLANES, :] = tail
    cnew_ref[...] = tail


def _next_tile(b, nl):
    def index(bi, li, *_):
        nxt = jnp.minimum(bi * nl + li + 1, b * nl - 1)
        return nxt // nl, nxt % nl, 0
    return index


def _lru(xn3, w_b, gates3, h0, c0p, cw8, cb, wax, ba, bx, lam, lt):
    b, L, _ = xn3.shape
    nl = L // lt
    kern = functools.partial(_lru_kernel, lt=lt)
    const = lambda bi, li: (0, 0)
    return pl.pallas_call(
        kern,
        grid=(b, nl),
        in_specs=[
            pl.BlockSpec((None, lt, D_MODEL), lambda bi, li: (bi, li, 0)),
            pl.BlockSpec((None, lt, D_MODEL), _next_tile(b, nl)),
            pl.BlockSpec((D_MODEL, D_MODEL), lambda bi, li: (0, B_SEC_XL), pipeline_mode=pl.Buffered(1)),
            pl.BlockSpec((None, lt, D_MODEL), lambda bi, li: (bi, li, G_SEC_GY)),
            pl.BlockSpec((None, 1, D_MODEL), lambda bi, li: (bi, 0, 0)),
            pl.BlockSpec((None, SUBLANES, D_MODEL), lambda bi, li: (bi, 0, 0)),
            pl.BlockSpec((SUBLANES, D_MODEL), const),
            pl.BlockSpec((1, D_MODEL), const),
            pl.BlockSpec((LRU_BLOCKS, LRU_BLOCK_WIDTH, 2 * LRU_BLOCK_WIDTH), lambda bi, li: (0, 0, 0)),
            pl.BlockSpec((1, D_MODEL), const),
            pl.BlockSpec((1, D_MODEL), const),
            pl.BlockSpec((1, D_MODEL), const),
        ],
        out_specs=[
            pl.BlockSpec((None, lt, D_MODEL), lambda bi, li: (bi, li, 0)),
            pl.BlockSpec((None, 1, D_MODEL), lambda bi, li: (bi, 0, 0)),
            pl.BlockSpec((None, SUBLANES, D_MODEL), lambda bi, li: (bi, 0, 0)),
        ],
        out_shape=[
            jax.ShapeDtypeStruct((b, L, D_MODEL), BF16),
            jax.ShapeDtypeStruct((b, 1, D_MODEL), F32),
            jax.ShapeDtypeStruct((b, SUBLANES, D_MODEL), F32),
        ],
        scratch_shapes=[
            pltpu.VMEM((lt + SUBLANES, D_MODEL), F32),
            pltpu.VMEM((lt, D_MODEL), F32),
            pltpu.VMEM((1, D_MODEL), F32),
        ],
        compiler_params=_cparams(("arbitrary", "arbitrary")),
        name="lru",
    )(xn3, xn3, w_b, gates3, h0, c0p, cw8, cb, wax, ba, bx, lam)


def _route(logits):
    neg = -1e30
    lane = lax.broadcasted_iota(jnp.int32, logits.shape, 1).astype(F32)
    is_g = lane < N_EXPERT_GROUPS
    gl = jnp.where(is_g, logits, neg)
    gmax = jnp.max(gl, axis=1, keepdims=True)
    gsum = jnp.sum(jnp.where(is_g, jnp.exp(gl - gmax), 0.0), axis=1, keepdims=True)
    g_w = 1.0 / gsum
    gidx = jnp.min(jnp.where(gl == gmax, lane, float(LANES)), axis=1, keepdims=True)
    lo = N_EXPERT_GROUPS + EXPERTS_PER_GROUP * gidx
    el = jnp.where((lane >= lo) & (lane < lo + EXPERTS_PER_GROUP), logits, neg)
    m1 = jnp.max(el, axis=1, keepdims=True)
    i1 = jnp.min(jnp.where(el == m1, lane, float(LANES)), axis=1, keepdims=True)
    el2 = jnp.where(lane == i1, neg, el)
    m2 = jnp.max(el2, axis=1, keepdims=True)
    i2 = jnp.min(jnp.where(el2 == m2, lane, float(LANES)), axis=1, keepdims=True)
    e2 = jnp.exp(m2 - m1)
    w1 = g_w / (1.0 + e2)
    w2 = g_w * e2 / (1.0 + e2)
    out = jnp.where(lane == 0, i1 - N_EXPERT_GROUPS, 0.0)
    out = jnp.where(lane == 1, i2 - N_EXPERT_GROUPS, out)
    out = jnp.where(lane == 2, w1, out)
    out = jnp.where(lane == 3, w2, out)
    return out


def _merge_kernel(x_ref, y_ref, hg_ref, gm_ref, gr_ref, wom_ref, wor_ref, wo_ref, gffn_ref, wr_ref, br_ref,
                  *rest, n_real):
    x1_ref, xn2_ref, route_ref = rest[-3:]
    i = pl.program_id(0)

    @pl.when(i < n_real)
    def _():
        o_m = _dot(y_ref[...], wom_ref[...])
        o_r = _dot(hg_ref[...], wor_ref[...])
        mixed = gm_ref[...] * o_m + gr_ref[...] * o_r
        x1 = x_ref[...] + _dot(mixed.astype(BF16), wo_ref[...])
        x1_ref[...] = x1
        xn2 = _rms(x1, gffn_ref[...])
        x_hi = xn2.astype(BF16)
        x_hi32 = x_hi.astype(F32)
        bits = lax.bitcast_convert_type(x_hi32, U32)
        xn2_ref[...] = (bits[:, :HALF_D] >> 16) | (bits[:, HALF_D:] & jnp.uint32(0xFFFF0000))
        x_lo = (xn2 - x_hi32).astype(BF16)
        logits = _dot(x_hi, wr_ref[0]) + _dot(x_lo, wr_ref[0]) + _dot(x_hi, wr_ref[1]) + br_ref[...]
        route_ref[...] = _route(logits)

    @pl.when(i >= n_real)
    def _():
        xn2_ref[...] = jnp.zeros_like(xn2_ref)
        route_ref[...] = jnp.zeros_like(route_ref)


def _merge(x2d, y2d, hg2d, proj2d, wom, wor, wo, gffn, wr, br, tm, t_all, row_off, shared):
    t = x2d.shape[0]
    n_real = t // tm
    off = row_off // tm
    n_extra = 0
    if shared is None and t_all > t:
        assert row_off == 0 and t_all - t <= tm
        n_extra = 1
    const = lambda i: (0, 0)
    cl = lambda i: jnp.minimum(i, n_real - 1)
    wspec = pl.BlockSpec((D_MODEL, D_MODEL), const, pipeline_mode=pl.Buffered(1))
    in_specs = [
        pl.BlockSpec((tm, D_MODEL), lambda i: (cl(i), 0)),
        pl.BlockSpec((tm, D_MODEL), lambda i: (cl(i), 0)),
        pl.BlockSpec((tm, D_MODEL), lambda i: (cl(i), 0)),
        pl.BlockSpec((tm, D_MODEL), lambda i: (cl(i), G_SEC_GM)),
        pl.BlockSpec((tm, D_MODEL), lambda i: (cl(i), G_SEC_GR)),
        wspec, wspec, wspec,
        pl.BlockSpec((1, D_MODEL), const),
        pl.BlockSpec((2, D_MODEL, LANES), lambda i: (0, 0, 0)),
        pl.BlockSpec((1, LANES), const),
    ]
    args = [x2d, y2d, hg2d, proj2d, proj2d, wom, wor, wo, gffn, wr, br]
    aliases = {}
    if shared is not None:
        in_specs += [pl.BlockSpec(memory_space=pl.ANY), pl.BlockSpec(memory_space=pl.ANY)]
        aliases = {len(args): 1, len(args) + 1: 2}
        args += list(shared)
    return pl.pallas_call(
        functools.partial(_merge_kernel, n_real=n_real),
        grid=(n_real + n_extra,),
        in_specs=in_specs,
        out_specs=[
            pl.BlockSpec((tm, D_MODEL), lambda i: (cl(i), 0)),
            pl.BlockSpec((tm, HALF_D), lambda i: (i + off, 0)),
            pl.BlockSpec((tm, LANES), lambda i: (i + off, 0)),
        ],
        out_shape=[
            jax.ShapeDtypeStruct((t, D_MODEL), F32),
            jax.ShapeDtypeStruct((t_all, HALF_D), U32),
            jax.ShapeDtypeStruct((t_all, LANES), F32),
        ],
        input_output_aliases=aliases,
        compiler_params=_cparams(("arbitrary",)),
        name="merge",
    )(*args)


def _moe_kernel(be_ref, nused_ref, tok_cur_ref, tok_nxt_ref, xn2_hbm, wg_ref, wu_ref, wd_ref, ys_ref,
                xbuf, wgb, wub, wdb, sem):
    i = pl.program_id(0)
    n_used = nused_ref[0]
    slot = i % 2

    def start_row(tok_ref, dst_slot, r):
        pltpu.make_async_copy(xn2_hbm.at[pl.ds(tok_ref[0, r], 1), :], xbuf.at[dst_slot, pl.ds(r, 1), :],
                              sem.at[dst_slot]).start()

    def compute(prefetch):
        pltpu.make_async_copy(xn2_hbm.at[pl.ds(0, MOE_ROWS), :], xbuf.at[slot], sem.at[slot]).wait()
        words = xbuf[slot]
        xa = lax.bitcast_convert_type(words << 16, F32).astype(BF16)
        xb = lax.bitcast_convert_type(words & jnp.uint32(0xFFFF0000), F32).astype(BF16)
        third = MOE_ROWS // 3

        def prefetch_rows(r0, r1):
            if prefetch:
                for r in range(r0, r1):
                    start_row(tok_nxt_ref, 1 - slot, r)

        prefetch_rows(0, third)
        hg = _dot(xa, wgb[0:HALF_D, :]) + _dot(xb, wgb[HALF_D:, :])
        prefetch_rows(third, 2 * third)
        hu = _dot(xa, wub[0:HALF_D, :]) + _dot(xb, wub[HALF_D:, :])
        prefetch_rows(2 * third, MOE_ROWS)
        ys_ref[...] = _dot((_silu(hg) * hu).astype(BF16), wdb[...])

    fresh = jnp.logical_or(i == 0, be_ref[i] != be_ref[jnp.maximum(i - 1, 0)])

    @pl.when(jnp.logical_and(fresh, i < n_used))
    def _():
        wgb[...] = wg_ref[...].astype(BF16)
        wub[...] = wu_ref[...].astype(BF16)
        wdb[...] = wd_ref[...].astype(BF16)

    @pl.when(jnp.logical_and(i == 0, n_used > 0))
    def _():
        def issue(r, c):
            start_row(tok_cur_ref, 0, r)
            return c
        lax.fori_loop(0, MOE_ROWS, issue, 0)

    @pl.when(i + 1 < n_used)
    def _():
        compute(True)

    @pl.when(jnp.logical_and(i < n_used, i + 1 >= n_used))
    def _():
        compute(False)

    @pl.when(i >= n_used)
    def _():
        ys_ref[...] = jnp.zeros_like(ys_ref)


def _moe(block_expert, n_used, row_token3, xn2_all, wg, wu, wd):
    n_blocks = row_token3.shape[0]
    grid_spec = pltpu.PrefetchScalarGridSpec(
        num_scalar_prefetch=2,
        grid=(n_blocks,),
        in_specs=[
            pl.BlockSpec((None, 1, MOE_ROWS), lambda i, be, nu: (i, 0, 0), memory_space=pltpu.SMEM),
            pl.BlockSpec((None, 1, MOE_ROWS), lambda i, be, nu: (jnp.minimum(i + 1, n_blocks - 1), 0, 0),
                         memory_space=pltpu.SMEM),
            pl.BlockSpec(memory_space=pl.ANY),
            pl.BlockSpec((None, D_MODEL, EXPERT_FF), lambda i, be, nu: (be[i], 0, 0)),
            pl.BlockSpec((None, D_MODEL, EXPERT_FF), lambda i, be, nu: (be[i], 0, 0)),
            pl.BlockSpec((None, EXPERT_FF, D_MODEL), lambda i, be, nu: (be[i], 0, 0)),
        ],
        out_specs=pl.BlockSpec((MOE_ROWS, D_MODEL), lambda i, be, nu: (i, 0)),
        scratch_shapes=[
            pltpu.VMEM((2, MOE_ROWS, HALF_D), U32),
            pltpu.VMEM((D_MODEL, EXPERT_FF), BF16),
            pltpu.VMEM((D_MODEL, EXPERT_FF), BF16),
            pltpu.VMEM((EXPERT_FF, D_MODEL), BF16),
            pltpu.SemaphoreType.DMA((2,)),
        ],
    )
    return pl.pallas_call(
        _moe_kernel,
        grid_spec=grid_spec,
        out_shape=jax.ShapeDtypeStruct((n_blocks * MOE_ROWS, D_MODEL), F32),
        compiler_params=_cparams(("arbitrary",)),
        name="moe",
    )(block_expert, n_used, row_token3, row_token3, xn2_all, wg, wu, wd)


def _final_kernel(d_cur_ref, d_nxt_ref, ys_hbm, x1_ref, route_ref, p_ref, gple_ref, wpg_ref, wple_ref, gfin_ref,
                  o_ref, gbuf, sem, *, tm):
    i = pl.program_id(0)
    n = pl.num_programs(0)
    slot = i % 2

    def start_row(d_ref, dst_slot, r):
        pltpu.make_async_copy(ys_hbm.at[pl.ds(d_ref[0, r], 1), :], gbuf.at[dst_slot, pl.ds(r, 1), :],
                              sem.at[dst_slot]).start()

    def compute(prefetch):
        pltpu.make_async_copy(ys_hbm.at[pl.ds(0, TOP_K * tm), :], gbuf.at[slot], sem.at[slot]).wait()
        route = route_ref[...]
        moe = route[:, 2:3] * gbuf[slot, 0:tm, :] + route[:, 3:4] * gbuf[slot, tm:2 * tm, :]
        x2 = x1_ref[...] + moe
        xn3 = _rms(x2, gple_ref[...]).astype(BF16)
        if prefetch:
            for r in range(TOP_K * tm):
                start_row(d_nxt_ref, 1 - slot, r)
        gate = jax.nn.sigmoid(_dot(xn3, wpg_ref[...]))
        x3 = x2 + gate * _dot(p_ref[...].astype(BF16), wple_ref[...])
        o_ref[...] = _rms(x3, gfin_ref[...])

    @pl.when(i == 0)
    def _():
        def issue(r, c):
            start_row(d_cur_ref, 0, r)
            return c
        lax.fori_loop(0, TOP_K * tm, issue, 0)

    @pl.when(i + 1 < n)
    def _():
        compute(True)

    @pl.when(i + 1 >= n)
    def _():
        compute(False)


def _final(dest3, ys, x1, route_all, p2d, gple, wpg, wple, gfin, tm, row_off):
    t = x1.shape[0]
    n = t // tm
    off = row_off // tm
    const = lambda i: (0, 0)
    kern = functools.partial(_final_kernel, tm=tm)
    return pl.pallas_call(
        kern,
        grid=(n,),
        in_specs=[
            pl.BlockSpec((None, 1, TOP_K * tm), lambda i: (i, 0, 0), memory_space=pltpu.SMEM),
            pl.BlockSpec((None, 1, TOP_K * tm), lambda i: (jnp.minimum(i + 1, n - 1), 0, 0), memory_space=pltpu.SMEM),
            pl.BlockSpec(memory_space=pl.ANY),
            pl.BlockSpec((tm, D_MODEL), lambda i: (i, 0)),
            pl.BlockSpec((tm, LANES), lambda i: (i + off, 0)),
            pl.BlockSpec((tm, PLE_DIM), lambda i: (i, 0)),
            pl.BlockSpec((1, D_MODEL), const),
            pl.BlockSpec((D_MODEL, D_MODEL), const, pipeline_mode=pl.Buffered(1)),
            pl.BlockSpec((PLE_DIM, D_MODEL), const),
            pl.BlockSpec((1, D_MODEL), const),
        ],
        out_specs=pl.BlockSpec((tm, D_MODEL), lambda i: (i, 0)),
        out_shape=jax.ShapeDtypeStruct((t, D_MODEL), F32),
        scratch_shapes=[
            pltpu.VMEM((2, TOP_K * tm, D_MODEL), F32),
            pltpu.SemaphoreType.DMA((2,)),
        ],
        compiler_params=_cparams(("arbitrary",)),
        name="final",
    )(dest3, dest3, ys, x1, route_all, p2d, gple, wpg, wple, gfin)


def _pad_rows8(a, at_end=False):
    r = a.shape[-2]
    pad = [(0, 0)] * (a.ndim - 2) + ([(0, SUBLANES - r)] if at_end else [(SUBLANES - r, 0)]) + [(0, 0)]
    return jnp.pad(a, pad)


def _dispatch(route_all):
    t_all = route_all.shape[0]
    n_assign = t_all * TOP_K
    n_blocks = -(-n_assign // MOE_ROWS) + N_EXPERTS
    ids = jnp.arange(N_EXPERTS, dtype=jnp.int32)[None, :]
    onehots = [(route_all[:, k].astype(jnp.int32)[:, None] == ids).astype(jnp.int32) for k in range(TOP_K)]
    per_token = onehots[0] + onehots[1]
    csum = jnp.cumsum(per_token, axis=0)
    before = csum - per_token
    counts = csum[-1]
    nblk = (counts + MOE_ROWS - 1) // MOE_ROWS
    blk_end = jnp.cumsum(nblk)
    blk_start = blk_end - nblk
    slot0 = blk_start[None, :] * MOE_ROWS + before
    dest = jnp.stack([jnp.sum(oh * slot0, axis=1) for oh in onehots])
    token = jnp.tile(jnp.arange(t_all, dtype=jnp.int32), TOP_K)
    row_token = jnp.zeros((n_blocks * MOE_ROWS,), jnp.int32).at[dest.reshape(-1)].set(token)
    block_expert = jnp.minimum(jnp.sum(blk_end[None, :] <= jnp.arange(n_blocks, dtype=jnp.int32)[:, None], axis=1),
                               N_EXPERTS - 1).astype(jnp.int32)
    n_used = blk_end[-1:].astype(jnp.int32)
    return dest.astype(jnp.int32), row_token.reshape(n_blocks, 1, MOE_ROWS), block_expert, n_used


def _tile_dest(dest, tm):
    t = dest.shape[1]
    return dest.reshape(TOP_K, t // tm, tm).transpose(1, 0, 2).reshape(t // tm, 1, TOP_K * tm)


def kernel(x_prompt, x_sample, state_ssm, state_ssm_conv, state_lru, state_lru_conv, p_prompt, p_sample, g_mix, w_in, conv_m_w, conv_m_b, dt_bias, a_log, d_skip, g_ssm, w_out_m, conv_r_w, conv_r_b, w_lru_a, b_lru_a, w_lru_x, b_lru_x, lru_lambda, w_out_r, w_o, g_ffn, w_router_group, b_router_group, w_router_expert, b_router_expert, w_exp_gate, w_exp_up, w_exp_down, g_ple, w_ple_gate, w_ple, g_final):
    depth = w_in.shape[0]
    assert depth == 1, "one layer per step"
    bp, lp_, _ = x_prompt.shape
    bs, ls, _ = x_sample.shape
    tp, ts = bp * lp_, bs * ls
    t_all = tp + ts

    w = w_in[0]
    assert w.shape == (D_MODEL, IN_DIM)
    w_a = jnp.concatenate([w[:, 0:OFF_XS], w[:, OFF_BC:OFF_BC + PROJ_A_TAIL]], axis=1).astype(BF16)
    w_xs = w[:, OFF_XS:OFF_BC].astype(BF16)
    w_b = w[:, OFF_GY:IN_DIM].astype(BF16)
    row = lambda v: v.reshape(1, -1).astype(F32)
    pad_lanes = lambda v: jnp.pad(v.reshape(1, -1).astype(F32), ((0, 0), (0, LANES - v.shape[-1])))
    cw_m = _pad_rows8(conv_m_w[0].astype(F32), at_end=True)
    cw_r = _pad_rows8(conv_r_w[0].astype(F32), at_end=True)
    dskip_rep = jnp.repeat(d_skip[0].astype(F32), SSM_HEAD_DIM).reshape(1, D_MODEL)
    rexp = (jnp.arange(D_MODEL)[None, :] // SSM_HEAD_DIM == jnp.arange(LANES)[:, None]).astype(BF16)
    tril = jnp.tril(jnp.ones((SSD_CHUNK, SSD_CHUNK), F32)).astype(BF16)
    wax = jnp.concatenate([w_lru_a[0], w_lru_x[0]], axis=-1).astype(BF16)
    wom, wor, wo = w_out_m[0].astype(BF16), w_out_r[0].astype(BF16), w_o[0].astype(BF16)
    wr32 = jnp.pad(jnp.concatenate([w_router_group[0], w_router_expert[0]], axis=1).astype(F32),
                   ((0, 0), (0, LANES - N_EXPERT_GROUPS - N_EXPERTS)))
    wr_hi = wr32.astype(BF16)
    wr = jnp.stack([wr_hi, (wr32 - wr_hi.astype(F32)).astype(BF16)])
    br = pad_lanes(jnp.concatenate([b_router_group[0], b_router_expert[0]]))
    wg, wu, wd = w_exp_gate[0].astype(F32), w_exp_up[0].astype(F32), w_exp_down[0].astype(F32)
    wpg, wple = w_ple_gate[0].astype(BF16), w_ple[0].astype(BF16)

    def branches(x, ssm0, sconv0, lru0, lconv0, tm_proj, lt):
        b, L, _ = x.shape
        x2d = x.reshape(b * L, D_MODEL)
        proj_a, xn = _proj_a(x2d, row(g_mix[0]), w_a, tm_proj)
        gates = _proj_gates(xn, w_b, min(2 * tm_proj, b * L))
        xn3 = xn.reshape(b, L, D_MODEL)
        y, ssm_new, sconv8 = _ssd(xn3, w_xs, proj_a.reshape(b, L, PROJ_A_W),
                                  ssm0.reshape(b, D_MODEL, SSM_STATE).astype(F32), _pad_rows8(sconv0.astype(F32)),
                                  cw_m, row(conv_m_b[0]), pad_lanes(dt_bias[0]), pad_lanes(a_log[0]), dskip_rep,
                                  row(g_ssm[0]), rexp, tril, lt)
        hg, lru_new, lconv8 = _lru(xn3, w_b, gates.reshape(b, L, 3 * D_MODEL),
                                   lru0.reshape(b, 1, D_MODEL).astype(F32), _pad_rows8(lconv0.astype(F32)),
                                   cw_r, row(conv_r_b[0]), wax, row(b_lru_a[0]), row(b_lru_x[0]), row(lru_lambda[0]), lt)
        states = (ssm_new.reshape(1, b, SSM_HEADS, SSM_HEAD_DIM, SSM_STATE),
                  sconv8[None, :, SUBLANES - (CONV_WIDTH - 1):, :],
                  lru_new.reshape(1, b, D_MODEL),
                  lconv8[None, :, SUBLANES - (CONV_WIDTH - 1):, :])
        return x2d, gates, y.reshape(b * L, D_MODEL), hg.reshape(b * L, D_MODEL), states

    zeros = lambda *s: jnp.zeros(s, F32)
    lt_p = min(256, lp_)
    tm_p = min(256, tp)
    tm_s = min(128, ts)
    xp2, proj_p, y_p, hg_p, st_p = branches(
        x_prompt, zeros(bp, SSM_HEADS, SSM_HEAD_DIM, SSM_STATE), zeros(bp, CONV_WIDTH - 1, SSM_CONV_DIM),
        zeros(bp, D_MODEL), zeros(bp, CONV_WIDTH - 1, D_MODEL), min(512, tp), lt_p)
    xs2, proj_s, y_s, hg_s, st_s = branches(
        x_sample, state_ssm[0], state_ssm_conv[0], state_lru[0], state_lru_conv[0], tm_s, ls)

    x1_p, xn2_all, route_all = _merge(xp2, y_p, hg_p, proj_p, wom, wor, wo, row(g_ffn[0]), wr, br, tm_p, t_all, 0, None)
    x1_s, xn2_all, route_all = _merge(xs2, y_s, hg_s, proj_s, wom, wor, wo, row(g_ffn[0]), wr, br, tm_s, t_all, tp,
                                      (xn2_all, route_all))

    dest, row_token3, block_expert, n_used = _dispatch(route_all)
    ys = _moe(block_expert, n_used, row_token3, xn2_all, wg, wu, wd)

    fin = lambda x1, d, p, tm, off: _final(_tile_dest(d, tm), ys, x1, route_all, p.reshape(-1, PLE_DIM).astype(F32),
                                           row(g_ple[0]), wpg, wple, row(g_final), tm, off)
    y_prompt = fin(x1_p, dest[:, :tp], p_prompt[0], tm_p, 0).reshape(x_prompt.shape)
    y_sample = fin(x1_s, dest[:, tp:], p_sample[0], tm_s, tp).reshape(x_sample.shape)
    return (y_prompt, y_sample) + st_p + st_s
```
